```python
import jax, jax.numpy as jnp
from jax import lax
import numpy as np


D_MODEL = 2048
BATCH = 2
SEQ = 8192
DEPTH = 1
DEC_BATCH = 8
DEC_SEQ = 32
PAST_LEN = 2048

CHUNK = 64
EPS = 1e-6
D_CONV = D_MODEL
CONV_W = 3
M_HEADS = 4
M_DV = D_MODEL // M_HEADS
M_DQK = M_DV // 2
N_EXPERTS = 32
TOP_K = 4
D_FF = D_MODEL
SWIGLU_ALPHA = 1.702
SWIGLU_LIMIT = 7.0
MOE_BLOCK = 128

PROJ_SIZES = (D_CONV, D_CONV, D_CONV, M_HEADS * M_DQK, M_HEADS * M_DQK, M_HEADS * M_DV, M_HEADS * M_DV,
              M_HEADS, M_HEADS, D_MODEL, D_MODEL)
D_PROJ = sum(PROJ_SIZES)
SPLIT_POINTS = tuple(int(s) for s in np.cumsum(PROJ_SIZES)[:-1])
F_GATE_OFF = sum(PROJ_SIZES[:8])

kernel_name = "hybrid_conv_mlstm_moe_adaln_stream_step"


def _rmsnorm(x, g):
    xf = x.astype(jnp.float32)
    y = xf * lax.rsqrt(jnp.mean(xf * xf, axis=-1, keepdims=True) + EPS)
    return (y * g.astype(jnp.float32)).astype(x.dtype)


def _causal_conv(u, state, w):
    L = u.shape[1]
    full = jnp.concatenate([state.astype(u.dtype), u], axis=1)
    y = full[:, 0:L] * w[0]
    for j in range(1, CONV_W):
        y = y + full[:, j:j + L] * w[j]
    return y, full[:, L:]


def _mlstm_chunk(carry, inp):
    C, n, m = carry
    q, k, v, ig, lf = inp
    L = q.shape[2]
    F = jnp.cumsum(lf, axis=-1)
    causal = jnp.tril(jnp.ones((L, L), dtype=bool))
    d = jnp.where(causal, F[..., :, None] - F[..., None, :] + ig[..., None, :], -jnp.inf)
    inter = F + m[..., None]
    m_t = jnp.maximum(inter, jnp.max(d, axis=-1))
    w_inter = jnp.exp(inter - m_t)
    s = jnp.einsum("bhtd,bhsd->bhts", q, k) * jnp.exp(d - m_t[..., None])
    num = w_inter[..., None] * jnp.einsum("bhtd,bhde->bhte", q, C) + jnp.einsum("bhts,bhse->bhte", s, v)
    den = w_inter * jnp.einsum("bhtd,bhd->bht", q, n) + jnp.sum(s, axis=-1)
    h = num / jnp.maximum(jnp.abs(den), jnp.exp(-m_t))[..., None]
    m_new = m_t[..., -1]
    decay = jnp.exp(F[..., -1] + m - m_new)
    w_s = jnp.exp(F[..., -1:] - F + ig - m_new[..., None])
    C_new = decay[..., None, None] * C + jnp.einsum("bhs,bhsd,bhse->bhde", w_s, k, v)
    n_new = decay[..., None] * n + jnp.einsum("bhs,bhsd->bhd", w_s, k)
    return (C_new, n_new, m_new), h


def _token_mixers(h, conv_state, C0, n0, m0, lp, chunk_len):
    B, L, _ = h.shape
    f32 = jnp.float32
    proj = h @ lp["w_in"] + lp["b_in"]
    bg, cg, xin, q, k, v, o_pre, i_pre, f_pre, g_conv, g_mlstm = jnp.split(proj, SPLIT_POINTS, axis=-1)
    y_conv, conv_new = _causal_conv(cg * xin, conv_state, lp["conv_w"])
    p_conv = (bg * y_conv) @ lp["w_conv_out"]
    def heads(a, dh):
        return a.reshape(B, L, M_HEADS, dh).transpose(0, 2, 1, 3).astype(f32)
    qh = heads(q, M_DQK)
    kh = heads(k, M_DQK) * (M_DQK ** -0.5)
    vh = heads(v, M_DV)
    ig = i_pre.astype(f32).transpose(0, 2, 1)
    lf = jax.nn.log_sigmoid(f_pre.astype(f32)).transpose(0, 2, 1)
    nc = L // chunk_len
    def to_chunks(a):
        return jnp.moveaxis(a.reshape(B, M_HEADS, nc, chunk_len, *a.shape[3:]), 2, 0)
    xs = tuple(to_chunks(a) for a in (qh, kh, vh, ig, lf))
    carry0 = (C0.astype(f32), n0.astype(f32), m0.astype(f32))
    (C1, n1, m1), hc = lax.scan(_mlstm_chunk, carry0, xs)
    hm = jnp.moveaxis(hc, 0, 2).reshape(B, M_HEADS, L, M_DV)
    hm = hm * lax.rsqrt(jnp.mean(hm * hm, axis=-1, keepdims=True) + EPS)
    hm = hm.transpose(0, 2, 1, 3).reshape(B, L, M_HEADS * M_DV) * lp["mlstm_norm_g"].astype(f32)
    p_m = (hm.astype(h.dtype) * jax.nn.sigmoid(o_pre)) @ lp["w_mlstm_out"]
    merged = jax.nn.sigmoid(g_conv) * p_conv + jax.nn.sigmoid(g_mlstm) * p_m
    return merged @ lp["w_out"], conv_new, C1, n1, m1


def _moe(x, lp):
    B, L, D = x.shape
    T = B * L
    xt = x.reshape(T, D)
    logits = (xt @ lp["w_router"] + lp["b_router"]).astype(jnp.float32)
    top_v, top_i = lax.top_k(logits, TOP_K)
    gates = jax.nn.softmax(top_v, axis=-1)
    NA = T * TOP_K
    flat_e = top_i.reshape(-1)
    flat_tok = jnp.repeat(jnp.arange(T, dtype=jnp.int32), TOP_K)
    flat_w = gates.reshape(-1)
    order = jnp.argsort(flat_e)
    sorted_e = flat_e[order]
    counts = jnp.zeros((N_EXPERTS,), jnp.int32).at[flat_e].add(1)
    starts = jnp.cumsum(counts) - counts
    padded = ((counts + MOE_BLOCK - 1) // MOE_BLOCK) * MOE_BLOCK
    pends = jnp.cumsum(padded)
    pstarts = pends - padded
    dest = pstarts[sorted_e] + (jnp.arange(NA, dtype=jnp.int32) - starts[sorted_e])
    NB = -(-NA // MOE_BLOCK) + N_EXPERTS
    NR = NB * MOE_BLOCK
    row_tok = jnp.full((NR,), T, jnp.int32).at[dest].set(flat_tok[order])
    row_w = jnp.zeros((NR,), jnp.float32).at[dest].set(flat_w[order])
    block_e = jnp.minimum(jnp.searchsorted(pends, jnp.arange(NB, dtype=jnp.int32) * MOE_BLOCK, side="right"),
                          N_EXPERTS - 1)
    xpad = jnp.concatenate([xt, jnp.zeros((1, D), xt.dtype)], axis=0)
    w_gu, b_gu, w_dn, b_dn = lp["w_gate_up"], lp["b_gate_up"], lp["w_down"], lp["b_down"]

    def expert_block(args):
        tok, e = args
        gu = xpad[tok] @ w_gu[e] + b_gu[e]
        glu = jnp.minimum(gu[:, :D_FF], SWIGLU_LIMIT)
        lin = jnp.clip(gu[:, D_FF:], -SWIGLU_LIMIT, SWIGLU_LIMIT)
        act = glu * jax.nn.sigmoid(SWIGLU_ALPHA * glu) * (lin + 1)
        return act @ w_dn[e] + b_dn[e]

    yb = lax.map(expert_block, (row_tok.reshape(NB, MOE_BLOCK), block_e))
    out = jnp.zeros((T + 1, D), jnp.float32).at[row_tok].add(yb.reshape(NR, D).astype(jnp.float32) * row_w[:, None])
    return out[:T].reshape(B, L, D).astype(x.dtype)


def _layer(x, c, conv_state, C0, n0, m0, lp, chunk_len):
    mod = jax.nn.silu(c) @ lp["w_ada"] + lp["b_ada"]
    sh1, sc1, gt1, sh2, sc2, gt2 = [a[:, None, :] for a in jnp.split(mod, 6, axis=-1)]
    h = _rmsnorm(x, lp["norm1_g"]) * (1 + sc1) + sh1
    mix, conv_new, C1, n1, m1 = _token_mixers(h, conv_state, C0, n0, m0, lp, chunk_len)
    x = x + gt1 * mix
    h2 = _rmsnorm(x, lp["norm2_g"]) * (1 + sc2) + sh2
    x = x + gt2 * _moe(h2, lp)
    return x, conv_new, C1, n1, m1


def setup_inputs(seed: int = 0) -> dict:
    key = jax.random.key(seed)
    ks = jax.random.split(key, 32)
    f32 = jnp.float32
    D = D_MODEL

    def nrm(k, shape, scale):
        return jax.random.normal(k, shape, f32) * scale

    b_in = nrm(ks[11], (DEPTH, D_PROJ), 0.02)
    b_in = b_in.at[:, F_GATE_OFF:F_GATE_OFF + M_HEADS].add(jnp.linspace(3.0, 6.0, M_HEADS))
    return {
        "x_prompt": nrm(ks[0], (BATCH, SEQ, D), 1.0),
        "x_sample": nrm(ks[1], (DEC_BATCH, DEC_SEQ, D), 1.0),
        "c_prompt": nrm(ks[2], (BATCH, D), 1.0),
        "c_sample": nrm(ks[3], (DEC_BATCH, D), 1.0),
        "state_conv": nrm(ks[4], (DEPTH, DEC_BATCH, CONV_W - 1, D_CONV), 0.5),
        "state_C": nrm(ks[5], (DEPTH, DEC_BATCH, M_HEADS, M_DQK, M_DV), 0.1),
        "state_n": nrm(ks[6], (DEPTH, DEC_BATCH, M_HEADS, M_DQK), 0.1),
        "state_m": nrm(ks[7], (DEPTH, DEC_BATCH, M_HEADS), 0.5) + 1.0,
        "norm1_g": 1.0 + nrm(ks[8], (DEPTH, D), 0.02),
        "norm2_g": 1.0 + nrm(ks[9], (DEPTH, D), 0.02),
        "final_g": 1.0 + nrm(ks[10], (D,), 0.02),
        "w_ada": nrm(ks[12], (DEPTH, D, 6 * D), 0.5 * D ** -0.5),
        "b_ada": nrm(ks[13], (DEPTH, 6 * D), 0.02),
        "w_in": nrm(ks[14], (DEPTH, D, D_PROJ), D ** -0.5),
        "b_in": b_in,
        "conv_w": nrm(ks[15], (DEPTH, CONV_W, D_CONV), CONV_W ** -0.5),
        "w_conv_out": nrm(ks[16], (DEPTH, D_CONV, D), D_CONV ** -0.5),
        "w_mlstm_out": nrm(ks[17], (DEPTH, M_HEADS * M_DV, D), (M_HEADS * M_DV) ** -0.5),
        "mlstm_norm_g": 1.0 + nrm(ks[18], (DEPTH, M_HEADS * M_DV), 0.02),
        "w_out": nrm(ks[19], (DEPTH, D, D), D ** -0.5),
        "w_router": nrm(ks[20], (DEPTH, D, N_EXPERTS), D ** -0.5),
        "b_router": nrm(ks[21], (DEPTH, N_EXPERTS), 0.01),
        "w_gate_up": nrm(ks[22], (DEPTH, N_EXPERTS, D, 2 * D_FF), D ** -0.5),
        "b_gate_up": nrm(ks[23], (DEPTH, N_EXPERTS, 2 * D_FF), 0.02),
        "w_down": nrm(ks[24], (DEPTH, N_EXPERTS, D_FF, D), D_FF ** -0.5),
        "b_down": nrm(ks[25], (DEPTH, N_EXPERTS, D), 0.02),
    }


def reference(x_prompt, x_sample, c_prompt, c_sample, state_conv, state_C, state_n, state_m,
              norm1_g, norm2_g, final_g, w_ada, b_ada, w_in, b_in, conv_w, w_conv_out, w_mlstm_out,
              mlstm_norm_g, w_out, w_router, b_router, w_gate_up, b_gate_up, w_down, b_down):
    yp = x_prompt
    ys = x_sample
    Bp = x_prompt.shape[0]
    conv_p, C_p, n_p, m_p = [], [], [], []
    conv_s, C_s, n_s, m_s = [], [], [], []
    for l in range(DEPTH):
        lp = {
            "norm1_g": norm1_g[l], "norm2_g": norm2_g[l], "w_ada": w_ada[l], "b_ada": b_ada[l],
            "w_in": w_in[l], "b_in": b_in[l], "conv_w": conv_w[l], "w_conv_out": w_conv_out[l],
            "w_mlstm_out": w_mlstm_out[l], "mlstm_norm_g": mlstm_norm_g[l], "w_out": w_out[l],
            "w_router": w_router[l], "b_router": b_router[l], "w_gate_up": w_gate_up[l],
            "b_gate_up": b_gate_up[l], "w_down": w_down[l], "b_down": b_down[l],
        }
        z_conv = jnp.zeros((Bp, CONV_W - 1, D_CONV), x_prompt.dtype)
        z_C = jnp.zeros((Bp, M_HEADS, M_DQK, M_DV), jnp.float32)
        z_n = jnp.zeros((Bp, M_HEADS, M_DQK), jnp.float32)
        z_m = jnp.zeros((Bp, M_HEADS), jnp.float32)
        yp, cp, Cp, nvp, mvp = _layer(yp, c_prompt, z_conv, z_C, z_n, z_m, lp, CHUNK)
        ys, cs, Cs, nvs, mvs = _layer(ys, c_sample, state_conv[l], state_C[l], state_n[l], state_m[l], lp,
                                      x_sample.shape[1])
        conv_p.append(cp); C_p.append(Cp); n_p.append(nvp); m_p.append(mvp)
        conv_s.append(cs); C_s.append(Cs); n_s.append(nvs); m_s.append(mvs)
    yp = _rmsnorm(yp, final_g)
    ys = _rmsnorm(ys, final_g)
    return (yp, ys,
            jnp.stack(conv_p), jnp.stack(C_p), jnp.stack(n_p), jnp.stack(m_p),
            jnp.stack(conv_s), jnp.stack(C_s), jnp.stack(n_s), jnp.stack(m_s))
```

```python
import functools

import jax
import jax.numpy as jnp
from jax import lax
from jax.experimental import pallas as pl
from jax.experimental.pallas import tpu as pltpu

EPS = 1e-6
TOP_K = 4
SWIGLU_ALPHA = 1.702
SWIGLU_LIMIT = 7.0

LANES = 128
SUBLANES = 8
VMEM_LIMIT_BYTES = 56 * 1024 * 1024
MOE_ROWS = 512
MLSTM_CHUNK = 256
DMA_UNROLL = 8

f32 = jnp.float32
bf16 = jnp.bfloat16


def _tile(dim, pref, align=LANES):
    if dim <= pref:
        return dim
    t = (pref // align) * align
    while t >= align:
        if dim % t == 0:
            return t
        t -= align
    return dim


def _params(*sem):
    return pltpu.CompilerParams(dimension_semantics=sem, vmem_limit_bytes=VMEM_LIMIT_BYTES)


def _split(a):
    hi = a.astype(bf16)
    return hi, (a - hi.astype(f32)).astype(bf16)


_NN = (((1,), (0,)), ((), ()))
_NT = (((1,), (1,)), ((), ()))
_TN = (((0,), (0,)), ((), ()))


def _dot3(a, b, dims=_NN):
    (ah, al), (bh, bl) = a, b
    d = functools.partial(lax.dot_general, dimension_numbers=dims, preferred_element_type=f32)
    return d(ah, bh) + (d(ah, bl) + d(al, bh))


def _mm(a, b, precise, dims=_NN):
    if precise:
        return _dot3(_split(a), _split(b), dims)
    return lax.dot_general(a.astype(bf16), b.astype(bf16), dims, preferred_element_type=f32)


def _ada_kernel(c_ref, w_ref, b_ref, o_ref):
    c = c_ref[...]
    o_ref[...] = _mm(c * jax.nn.sigmoid(c), w_ref[...], True) + b_ref[...]


def _ada(c, w, b):
    m, d = c.shape
    n = w.shape[1]
    tn = _tile(n, 512)
    return pl.pallas_call(
        _ada_kernel,
        grid=(n // tn,),
        in_specs=[
            pl.BlockSpec((m, d), lambda j: (0, 0)),
            pl.BlockSpec((d, tn), lambda j: (0, j)),
            pl.BlockSpec((1, tn), lambda j: (0, j)),
        ],
        out_specs=pl.BlockSpec((m, tn), lambda j: (0, j)),
        out_shape=jax.ShapeDtypeStruct((m, n), f32),
        compiler_params=_params("arbitrary"),
        name="ada_mod",
    )(c, w, b.reshape(1, n))


def _norm1_kernel(x_ref, g_ref, sc_ref, sh_ref, wif_ref, bif_ref, h_ref, gif_ref, *, precise):
    bb, tl, d = x_ref.shape
    x = x_ref[...]
    y = x * lax.rsqrt(jnp.mean(x * x, axis=-1, keepdims=True) + EPS) * g_ref[...]
    h = y * (1.0 + sc_ref[...]) + sh_ref[...]
    h_ref[...] = h.astype(h_ref.dtype)
    gif = _mm(h.reshape(bb * tl, d), wif_ref[...], precise) + bif_ref[...]
    gif_ref[...] = gif.reshape(bb, tl, LANES)


def _norm1(x, g, sc, sh, wif, bif, bb, tl, precise):
    b, l, d = x.shape
    return pl.pallas_call(
        functools.partial(_norm1_kernel, precise=precise),
        grid=(b // bb, l // tl),
        in_specs=[
            pl.BlockSpec((bb, tl, d), lambda i, j: (i, j, 0)),
            pl.BlockSpec((1, 1, d), lambda i, j: (0, 0, 0)),
            pl.BlockSpec((bb, 1, d), lambda i, j: (i, 0, 0)),
            pl.BlockSpec((bb, 1, d), lambda i, j: (i, 0, 0)),
            pl.BlockSpec((d, LANES), lambda i, j: (0, 0)),
            pl.BlockSpec((1, LANES), lambda i, j: (0, 0)),
        ],
        out_specs=[
            pl.BlockSpec((bb, tl, d), lambda i, j: (i, j, 0)),
            pl.BlockSpec((bb, tl, LANES), lambda i, j: (i, j, 0)),
        ],
        out_shape=[jax.ShapeDtypeStruct((b, l, d), f32 if precise else bf16),
                   jax.ShapeDtypeStruct((b, l, LANES), f32)],
        compiler_params=_params("arbitrary", "arbitrary"),
        name="norm1",
    )(x, g.reshape(1, 1, d), sc, sh, wif, bif)


def _proj_kernel(h_ref, w_ref, b_ref, o_ref, whi_ref, *wlo_ref, precise):
    @pl.when(pl.program_id(1) == 0)
    def _():
        if precise:
            whi_ref[...], wlo_ref[0][...] = _split(w_ref[...])
        else:
            whi_ref[...] = w_ref[...].astype(bf16)

    if precise:
        acc = _dot3(_split(h_ref[...]), (whi_ref[...], wlo_ref[0][...]))
    else:
        acc = jnp.dot(h_ref[...], whi_ref[...], preferred_element_type=f32)
    o_ref[...] = (acc + b_ref[...]).astype(o_ref.dtype)


def _proj(h, w, b, n_cols, precise):
    r, d = h.shape
    tn = _tile(n_cols, 512 if precise else 1024)
    tm = _tile(r, 1024, SUBLANES)
    return pl.pallas_call(
        functools.partial(_proj_kernel, precise=precise),
        grid=(n_cols // tn, r // tm),
        in_specs=[
            pl.BlockSpec((tm, d), lambda j, i: (i, 0)),
            pl.BlockSpec((d, tn), lambda j, i: (0, j)),
            pl.BlockSpec((1, tn), lambda j, i: (0, j)),
        ],
        out_specs=pl.BlockSpec((tm, tn), lambda j, i: (i, j)),
        out_shape=jax.ShapeDtypeStruct((r, n_cols), f32),
        scratch_shapes=[pltpu.VMEM((d, tn), bf16)] * (2 if precise else 1),
        compiler_params=_params("arbitrary", "arbitrary"),
        name="proj",
    )(h, w, b.reshape(1, -1))


def _conv_kernel(bg_ref, cg_ref, xin_ref, st_ref, w_ref, z_ref, ns_ref, carry_ref):
    li = pl.program_id(2)
    tl = bg_ref.shape[1]
    kw = w_ref.shape[0]

    @pl.when(li == 0)
    def _():
        carry_ref[...] = jnp.zeros_like(carry_ref)
        carry_ref[SUBLANES - (kw - 1):, :] = st_ref[0]

    u = cg_ref[0] * xin_ref[0]
    row = lax.broadcasted_iota(jnp.int32, u.shape, 0)
    y = u * w_ref[kw - 1:kw, :]
    for s in range(1, kw):
        us = pltpu.roll(u, s, 0)
        for r in range(s):
            us = jnp.where(row == r, carry_ref[SUBLANES - s + r:SUBLANES - s + r + 1, :], us)
        y = y + us * w_ref[kw - 1 - s:kw - s, :]
    z_ref[0] = (bg_ref[0] * y).astype(z_ref.dtype)
    carry_ref[...] = u[tl - SUBLANES:, :]

    @pl.when(li == pl.num_programs(2) - 1)
    def _():
        ns_ref[0] = u[tl - (kw - 1):, :]


def _conv_mixer(pm, state, conv_w, dc, out_dtype):
    b, l, _ = pm.shape
    kw = conv_w.shape[0]
    tl = _tile(l, 1024, SUBLANES)
    tc = _tile(dc, 512)
    assert tl >= SUBLANES and tl % SUBLANES == 0 and kw - 1 <= SUBLANES and l >= kw - 1
    nc = dc // tc
    return pl.pallas_call(
        _conv_kernel,
        grid=(b, nc, l // tl),
        in_specs=[
            pl.BlockSpec((1, tl, tc), lambda i, c, j: (i, j, c)),
            pl.BlockSpec((1, tl, tc), lambda i, c, j: (i, j, c + nc)),
            pl.BlockSpec((1, tl, tc), lambda i, c, j: (i, j, c + 2 * nc)),
            pl.BlockSpec((1, kw - 1, tc), lambda i, c, j: (i, 0, c)),
            pl.BlockSpec((kw, tc), lambda i, c, j: (0, c)),
        ],
        out_specs=[
            pl.BlockSpec((1, tl, tc), lambda i, c, j: (i, j, c)),
            pl.BlockSpec((1, kw - 1, tc), lambda i, c, j: (i, 0, c)),
        ],
        out_shape=[jax.ShapeDtypeStruct((b, l, dc), out_dtype), jax.ShapeDtypeStruct((b, kw - 1, dc), f32)],
        scratch_shapes=[pltpu.VMEM((SUBLANES, tc), f32)],
        compiler_params=_params("arbitrary", "arbitrary", "arbitrary"),
        name="conv_mixer",
    )(pm, pm, pm, state, conv_w)


def _mlstm_kernel(q_ref, k_ref, v_ref, o_ref, gif_ref, c0_ref, n0_ref, m0_ref, ng_ref,
                  zm_ref, c1_ref, n1_ref, m1_ref, c_s, n_s, m_s, *, heads, dqk, dv, precise):
    ci = pl.program_id(1)
    lc = q_ref.shape[1]

    @pl.when(ci == 0)
    def _():
        c_s[...] = c0_ref[0]
        n_s[...] = n0_ref[0]
        m_s[...] = m0_ref[0]

    gif = gif_ref[0]
    lf = jnp.minimum(gif, 0.0) - jnp.log(1.0 + jnp.exp(-jnp.abs(gif)))
    row = lax.broadcasted_iota(jnp.int32, gif.shape, 0)
    fc = lf
    s = 1
    while s < lc:
        fc = fc + jnp.where(row >= s, pltpu.roll(fc, s, 0), 0.0)
        s *= 2
    gif_t = gif.T
    fc_t = fc.T
    tri = lax.broadcasted_iota(jnp.int32, (lc, lc), 0) >= lax.broadcasted_iota(jnp.int32, (lc, lc), 1)
    scale = dqk ** -0.5

    for h in range(heads):
        ig_col = gif[:, h:h + 1]
        ig_row = gif_t[h:h + 1, :]
        f_col = fc[:, heads + h:heads + h + 1]
        f_row = fc_t[heads + h:heads + h + 1, :]
        m_prev = m_s[:, h:h + 1]
        qf = q_ref[0, :, h * dqk:(h + 1) * dqk]
        kf = k_ref[0, :, h * dqk:(h + 1) * dqk] * scale
        vf = v_ref[0, :, h * dv:(h + 1) * dv]
        c_h = c_s[h]
        n_h = n_s[h:h + 1, :]

        d = jnp.where(tri, f_col - f_row + ig_row, -jnp.inf)
        inter = f_col + m_prev
        m_t = jnp.maximum(inter, jnp.max(d, axis=-1, keepdims=True))
        w_inter = jnp.exp(inter - m_t)
        p = jnp.exp(d - m_t)
        sm = _mm(qf, kf, precise, _NT) * p
        num = w_inter * _mm(qf, c_h, precise) + _mm(sm, vf, precise)
        den = w_inter * jnp.sum(qf * n_h, axis=-1, keepdims=True) + jnp.sum(sm, axis=-1, keepdims=True)
        hh = num / jnp.maximum(jnp.abs(den), jnp.exp(-m_t))

        f_last = f_col[lc - 1:lc, :]
        m_new = m_t[lc - 1:lc, :]
        decay = jnp.exp(f_last + m_prev - m_new)
        w_s = jnp.exp(f_last - f_col + ig_col - m_new)
        kw = kf * w_s
        c_s[h] = decay * c_h + _mm(kw, vf, precise, _TN)
        n_s[h:h + 1, :] = decay * n_h + jnp.sum(kw, axis=0, keepdims=True)
        m_s[:, h:h + 1] = m_new

        hn = hh * lax.rsqrt(jnp.mean(hh * hh, axis=-1, keepdims=True) + EPS) * ng_ref[:, h * dv:(h + 1) * dv]
        zm_ref[0, :, h * dv:(h + 1) * dv] = (hn * jax.nn.sigmoid(o_ref[0, :, h * dv:(h + 1) * dv])).astype(zm_ref.dtype)

    @pl.when(ci == pl.num_programs(1) - 1)
    def _():
        c1_ref[0] = c_s[...]
        n1_ref[0] = n_s[...]
        m1_ref[0] = m_s[...]


def _mlstm_mixer(pm, gif, c0, n0, m0, norm_g, off_q, heads, dqk, dv, precise):
    b, l, _ = pm.shape
    lc = _tile(l, MLSTM_CHUNK, SUBLANES)
    wq, wv = heads * dqk, heads * dv
    off_k, off_v, off_o = off_q + wq, off_q + 2 * wq, off_q + 2 * wq + wv
    assert off_q % wq == 0 and off_v % wv == 0 and off_o % wv == 0
    kern = functools.partial(_mlstm_kernel, heads=heads, dqk=dqk, dv=dv, precise=precise)
    return pl.pallas_call(
        kern,
        grid=(b, l // lc),
        in_specs=[
            pl.BlockSpec((1, lc, wq), lambda i, j: (i, j, off_q // wq)),
            pl.BlockSpec((1, lc, wq), lambda i, j: (i, j, off_k // wq)),
            pl.BlockSpec((1, lc, wv), lambda i, j: (i, j, off_v // wv)),
            pl.BlockSpec((1, lc, wv), lambda i, j: (i, j, off_o // wv)),
            pl.BlockSpec((1, lc, LANES), lambda i, j: (i, j, 0)),
            pl.BlockSpec((1, heads, dqk, dv), lambda i, j: (i, 0, 0, 0)),
            pl.BlockSpec((1, heads, dqk), lambda i, j: (i, 0, 0)),
            pl.BlockSpec((1, 1, heads), lambda i, j: (i, 0, 0)),
            pl.BlockSpec((1, wv), lambda i, j: (0, 0)),
        ],
        out_specs=[
            pl.BlockSpec((1, lc, wv), lambda i, j: (i, j, 0)),
            pl.BlockSpec((1, heads, dqk, dv), lambda i, j: (i, 0, 0, 0)),
            pl.BlockSpec((1, heads, dqk), lambda i, j: (i, 0, 0)),
            pl.BlockSpec((1, 1, heads), lambda i, j: (i, 0, 0)),
        ],
        out_shape=[
            jax.ShapeDtypeStruct((b, l, wv), f32 if precise else bf16),
            jax.ShapeDtypeStruct((b, heads, dqk, dv), f32),
            jax.ShapeDtypeStruct((b, heads, dqk), f32),
            jax.ShapeDtypeStruct((b, 1, heads), f32),
        ],
        scratch_shapes=[pltpu.VMEM((heads, dqk, dv), f32), pltpu.VMEM((heads, dqk), f32), pltpu.VMEM((1, heads), f32)],
        compiler_params=_params("arbitrary", "arbitrary"),
        name="mlstm_mixer",
    )(pm, pm, pm, pm, gif, c0, n0, m0.reshape(b, 1, heads), norm_g.reshape(1, wv))


def _merge_kernel(z_ref, zm_ref, gc_ref, gm_ref, wc_ref, wm_ref, o_ref, *, precise):
    if precise:
        pc = _mm(z_ref[...], wc_ref[...], True)
        pm = _mm(zm_ref[...], wm_ref[...], True)
    else:
        pc = jnp.dot(z_ref[...], wc_ref[...], preferred_element_type=f32)
        pm = jnp.dot(zm_ref[...], wm_ref[...], preferred_element_type=f32)
    o_ref[...] = (jax.nn.sigmoid(gc_ref[...]) * pc + jax.nn.sigmoid(gm_ref[...]) * pm).astype(o_ref.dtype)


def _merge(z, zm, g, wc, wm, precise):
    r, dc = z.shape
    dm = zm.shape[1]
    d = wc.shape[1]
    tn = _tile(d, 512)
    tm = _tile(r, 512, SUBLANES)
    nn = d // tn
    return pl.pallas_call(
        functools.partial(_merge_kernel, precise=precise),
        grid=(nn, r // tm),
        in_specs=[
            pl.BlockSpec((tm, dc), lambda j, i: (i, 0)),
            pl.BlockSpec((tm, dm), lambda j, i: (i, 0)),
            pl.BlockSpec((tm, tn), lambda j, i: (i, j)),
            pl.BlockSpec((tm, tn), lambda j, i: (i, j + nn)),
            pl.BlockSpec((dc, tn), lambda j, i: (0, j)),
            pl.BlockSpec((dm, tn), lambda j, i: (0, j)),
        ],
        out_specs=pl.BlockSpec((tm, tn), lambda j, i: (i, j)),
        out_shape=jax.ShapeDtypeStruct((r, d), f32 if precise else bf16),
        compiler_params=_params("arbitrary", "arbitrary"),
        name="merge",
    )(z, zm, g, g, wc, wm)


def _post_kernel(mg_ref, x_ref, gt_ref, g_ref, sc_ref, sh_ref, wr_ref, br_ref, *rest, has_mix):
    if has_mix:
        x1_ref, h2_ref, ti_ref, tg_ref = rest
    else:
        wo_ref, x1_ref, h2_ref, ti_ref, tg_ref = rest
    bb, tl, d = x_ref.shape
    if has_mix:
        mix = mg_ref[...]
    else:
        mix = jnp.dot(mg_ref[...].reshape(bb * tl, d), wo_ref[...], preferred_element_type=f32).reshape(bb, tl, d)
    x1 = x_ref[...] + gt_ref[...] * mix
    x1_ref[...] = x1
    y = x1 * lax.rsqrt(jnp.mean(x1 * x1, axis=-1, keepdims=True) + EPS) * g_ref[...]
    h2 = (y * (1.0 + sc_ref[...]) + sh_ref[...]).reshape(bb * tl, d)
    h2_ref[...] = h2.reshape(bb, tl, d)
    lg = _mm(h2, wr_ref[...], True) + br_ref[...]
    lane = lax.broadcasted_iota(jnp.int32, lg.shape, 1)
    vals, idxs = [], []
    for _ in range(TOP_K):
        mk = jnp.max(lg, axis=-1, keepdims=True)
        ik = jnp.min(jnp.where(lg == mk, lane, LANES), axis=-1, keepdims=True)
        vals.append(mk)
        idxs.append(ik)
        lg = jnp.where(lane == ik, -jnp.inf, lg)
    es = [jnp.exp(v - vals[0]) for v in vals]
    tot = es[0]
    for e in es[1:]:
        tot = tot + e
    ti = jnp.zeros(lg.shape, jnp.int32)
    tg = jnp.zeros(lg.shape, f32)
    for k in range(TOP_K):
        ti = jnp.where(lane == k, idxs[k], ti)
        tg = jnp.where(lane == k, es[k] / tot, tg)
    ti_ref[...] = ti.reshape(bb, tl, LANES)
    tg_ref[...] = tg.reshape(bb, tl, LANES)


def _post(mg, x, gt, g, sc, sh, wr, br, wo, bb, tl):
    b, l, d = x.shape
    row = lambda i, j: (i, j, 0)
    per_b = lambda i, j: (i, 0, 0)
    const2 = lambda i, j: (0, 0)
    in_specs = [
        pl.BlockSpec((bb, tl, d), row),
        pl.BlockSpec((bb, tl, d), row),
        pl.BlockSpec((bb, 1, d), per_b),
        pl.BlockSpec((1, 1, d), lambda i, j: (0, 0, 0)),
        pl.BlockSpec((bb, 1, d), per_b),
        pl.BlockSpec((bb, 1, d), per_b),
        pl.BlockSpec((d, LANES), const2),
        pl.BlockSpec((1, LANES), const2),
    ]
    args = [mg, x, gt, g.reshape(1, 1, d), sc, sh, wr, br]
    if wo is not None:
        in_specs.append(pl.BlockSpec((d, d), const2))
        args.append(wo)
    return pl.pallas_call(
        functools.partial(_post_kernel, has_mix=wo is None),
        grid=(b // bb, l // tl),
        in_specs=in_specs,
        out_specs=[
            pl.BlockSpec((bb, tl, d), row),
            pl.BlockSpec((bb, tl, d), row),
            pl.BlockSpec((bb, tl, LANES), row),
            pl.BlockSpec((bb, tl, LANES), row),
        ],
        out_shape=[
            jax.ShapeDtypeStruct((b, l, d), f32),
            jax.ShapeDtypeStruct((b, l, d), f32),
            jax.ShapeDtypeStruct((b, l, LANES), jnp.int32),
            jax.ShapeDtypeStruct((b, l, LANES), f32),
        ],
        compiler_params=_params("arbitrary", "arbitrary"),
        name="post_mixer",
    )(*args)


def _gather_rows(idx_ref, n_rows, src_hbm, dst_at, sem):
    def issue(i, c):
        pltpu.make_async_copy(src_hbm.at[pl.ds(idx_ref[0, 0, i], 1)], dst_at(i), sem).start()
        return c

    lax.fori_loop(0, n_rows, issue, 0, unroll=DMA_UNROLL)

    def drain(i, c):
        pltpu.make_async_copy(src_hbm.at[pl.ds(0, 1)], dst_at(0), sem).wait()
        return c

    lax.fori_loop(0, n_rows, drain, 0, unroll=DMA_UNROLL)


def _dispatch_kernel(tot_ref, tok_ref, h2_ref, xs_ref, buf_ref, sem):
    j = pl.program_id(0)
    tm = buf_ref.shape[0]

    @pl.when(j < tot_ref[0])
    def _():
        _gather_rows(tok_ref, tm, h2_ref, lambda i: buf_ref.at[pl.ds(i, 1)], sem)
        xs_ref[...] = buf_ref[...].astype(xs_ref.dtype)

    @pl.when(j >= tot_ref[0])
    def _():
        xs_ref[...] = jnp.zeros_like(xs_ref)


def _dispatch(h2, row_tok, total, tm, n_tiles):
    d = h2.shape[1]
    return pl.pallas_call(
        _dispatch_kernel,
        grid_spec=pltpu.PrefetchScalarGridSpec(
            num_scalar_prefetch=1,
            grid=(n_tiles,),
            in_specs=[
                pl.BlockSpec((1, 1, tm), lambda j, tot: (j, 0, 0), memory_space=pltpu.SMEM),
                pl.BlockSpec(memory_space=pl.ANY),
            ],
            out_specs=pl.BlockSpec((tm, d), lambda j, tot: (j, 0)),
            scratch_shapes=[pltpu.VMEM((tm, d), f32), pltpu.SemaphoreType.DMA],
        ),
        out_shape=jax.ShapeDtypeStruct((n_tiles * tm, d), bf16),
        compiler_params=_params("arbitrary"),
        name="moe_dispatch",
    )(total, row_tok.reshape(n_tiles, 1, tm), h2)


def _moe_up_kernel(te_ref, first_ref, tot_ref, xs_ref, wg_ref, wu_ref, bg_ref, bu_ref, act_ref, wg_s, wu_s):
    j = pl.program_id(1)

    @pl.when(j < tot_ref[0])
    def _():
        @pl.when(first_ref[j] == 1)
        def _():
            wg_s[...] = wg_ref[0].astype(bf16)
            wu_s[...] = wu_ref[0].astype(bf16)

        x = xs_ref[...]
        gate = jnp.dot(x, wg_s[...], preferred_element_type=f32) + bg_ref[0]
        up = jnp.dot(x, wu_s[...], preferred_element_type=f32) + bu_ref[0]
        glu = jnp.minimum(gate, SWIGLU_LIMIT)
        lin = jnp.clip(up, -SWIGLU_LIMIT, SWIGLU_LIMIT)
        act_ref[...] = (glu * jax.nn.sigmoid(SWIGLU_ALPHA * glu) * (lin + 1.0)).astype(act_ref.dtype)

    @pl.when(j >= tot_ref[0])
    def _():
        act_ref[...] = jnp.zeros_like(act_ref)


def _moe_up(xs, w_gu, b_gu, tile_e, first, total, tm, n_tiles, dff):
    d = xs.shape[1]
    experts = w_gu.shape[0]
    tf = _tile(dff, 512)
    nf = dff // tf
    last = lambda j, tot: jnp.minimum(j, tot[0] - 1)
    return pl.pallas_call(
        _moe_up_kernel,
        grid_spec=pltpu.PrefetchScalarGridSpec(
            num_scalar_prefetch=3,
            grid=(nf, n_tiles),
            in_specs=[
                pl.BlockSpec((tm, d), lambda f, j, te, fi, tot: (last(j, tot), 0)),
                pl.BlockSpec((1, d, tf), lambda f, j, te, fi, tot: (te[j], 0, f)),
                pl.BlockSpec((1, d, tf), lambda f, j, te, fi, tot: (te[j], 0, f + nf)),
                pl.BlockSpec((1, 1, tf), lambda f, j, te, fi, tot: (te[j], 0, f)),
                pl.BlockSpec((1, 1, tf), lambda f, j, te, fi, tot: (te[j], 0, f + nf)),
            ],
            out_specs=pl.BlockSpec((tm, tf), lambda f, j, te, fi, tot: (j, f)),
            scratch_shapes=[pltpu.VMEM((d, tf), bf16), pltpu.VMEM((d, tf), bf16)],
        ),
        out_shape=jax.ShapeDtypeStruct((n_tiles * tm, dff), bf16),
        compiler_params=_params("arbitrary", "arbitrary"),
        name="moe_up",
    )(tile_e, first, total, xs, w_gu, w_gu, b_gu.reshape(experts, 1, -1), b_gu.reshape(experts, 1, -1))


def _moe_down_kernel(te_ref, first_ref, tot_ref, act_ref, w_ref, b_ref, y_ref, w_s):
    j = pl.program_id(1)

    @pl.when(j < tot_ref[0])
    def _():
        @pl.when(first_ref[j] == 1)
        def _():
            w_s[...] = w_ref[0].astype(bf16)

        y_ref[...] = jnp.dot(act_ref[...], w_s[...], preferred_element_type=f32) + b_ref[0]

    @pl.when(j >= tot_ref[0])
    def _():
        y_ref[...] = jnp.zeros_like(y_ref)


def _moe_down(act, w_dn, b_dn, tile_e, first, total, tm, n_tiles):
    dff = act.shape[1]
    experts, _, d = w_dn.shape
    tn = _tile(d, 1024)
    last = lambda j, tot: jnp.minimum(j, tot[0] - 1)
    return pl.pallas_call(
        _moe_down_kernel,
        grid_spec=pltpu.PrefetchScalarGridSpec(
            num_scalar_prefetch=3,
            grid=(d // tn, n_tiles),
            in_specs=[
                pl.BlockSpec((tm, dff), lambda n, j, te, fi, tot: (last(j, tot), 0)),
                pl.BlockSpec((1, dff, tn), lambda n, j, te, fi, tot: (te[j], 0, n)),
                pl.BlockSpec((1, 1, tn), lambda n, j, te, fi, tot: (te[j], 0, n)),
            ],
            out_specs=pl.BlockSpec((tm, tn), lambda n, j, te, fi, tot: (j, n)),
            scratch_shapes=[pltpu.VMEM((dff, tn), bf16)],
        ),
        out_shape=jax.ShapeDtypeStruct((n_tiles * tm, d), f32),
        compiler_params=_params("arbitrary", "arbitrary"),
        name="moe_down",
    )(tile_e, first, total, act, w_dn, b_dn.reshape(experts, 1, d))


def _combine_kernel(dest_ref, x1_ref, tg_ref, gt_ref, fg_ref, yb_ref, o_ref, buf_ref, sem, *, final_norm):
    tl = x1_ref.shape[1]
    _gather_rows(dest_ref, TOP_K * tl, yb_ref, lambda i: buf_ref.at[pl.ds(i, 1)], sem)
    tg = tg_ref[0]
    moe = tg[:, 0:1] * buf_ref[0:tl, :]
    for k in range(1, TOP_K):
        moe = moe + tg[:, k:k + 1] * buf_ref[k * tl:(k + 1) * tl, :]
    xo = x1_ref[0] + gt_ref[0] * moe
    if final_norm:
        xo = xo * lax.rsqrt(jnp.mean(xo * xo, axis=-1, keepdims=True) + EPS) * fg_ref[...]
    o_ref[0] = xo


def _combine(x1, tg, gt, fg, yb, dest, final_norm):
    b, l, d = x1.shape
    tl = _tile(l, 128, SUBLANES)
    nl = l // tl
    dest_t = dest.reshape(b * nl, tl, TOP_K).transpose(0, 2, 1).reshape(b * nl, 1, TOP_K * tl)
    return pl.pallas_call(
        functools.partial(_combine_kernel, final_norm=final_norm),
        grid=(b, nl),
        in_specs=[
            pl.BlockSpec((1, 1, TOP_K * tl), lambda i, j: (i * nl + j, 0, 0), memory_space=pltpu.SMEM),
            pl.BlockSpec((1, tl, d), lambda i, j: (i, j, 0)),
            pl.BlockSpec((1, tl, LANES), lambda i, j: (i, j, 0)),
            pl.BlockSpec((1, 1, d), lambda i, j: (i, 0, 0)),
            pl.BlockSpec((1, d), lambda i, j: (0, 0)),
            pl.BlockSpec(memory_space=pl.ANY),
        ],
        out_specs=pl.BlockSpec((1, tl, d), lambda i, j: (i, j, 0)),
        out_shape=jax.ShapeDtypeStruct((b, l, d), f32),
        scratch_shapes=[pltpu.VMEM((TOP_K * tl, d), f32), pltpu.SemaphoreType.DMA],
        compiler_params=_params("arbitrary", "arbitrary"),
        name="moe_combine",
    )(dest_t, x1, tg, gt, fg.reshape(1, d), yb)


def _routing_plan(top_i, experts, tm, n_tiles):
    flat_e = top_i.reshape(-1)
    na = flat_e.shape[0]
    onehot = (flat_e[:, None] == jnp.arange(experts, dtype=jnp.int32)[None, :]).astype(jnp.int32)
    csum = jnp.cumsum(onehot, axis=0)
    rank = jnp.sum(csum * onehot, axis=1) - 1
    counts = csum[-1]
    padded = ((counts + tm - 1) // tm) * tm
    pends = jnp.cumsum(padded)
    pstart = pends - padded
    starts = jnp.cumsum(counts) - counts
    dest = (pstart[flat_e] + rank).astype(jnp.int32)
    total = (pends[-1] // tm).astype(jnp.int32)
    tile_id = jnp.minimum(jnp.arange(n_tiles, dtype=jnp.int32), total - 1)
    tile_e = jnp.minimum(jnp.searchsorted(pends, tile_id * tm, side="right"), experts - 1).astype(jnp.int32)
    first = jnp.concatenate([jnp.ones((1,), jnp.int32), (tile_e[1:] != tile_e[:-1]).astype(jnp.int32)])
    _, order = lax.sort((dest, jnp.arange(na, dtype=jnp.int32)), num_keys=1)
    rows = jnp.arange(n_tiles * tm, dtype=jnp.int32)
    row_e = tile_e[rows // tm]
    off = rows - pstart[row_e]
    valid = (off < counts[row_e]) & (rows < total * tm)
    row_tok = jnp.where(valid, order[jnp.clip(starts[row_e] + off, 0, na - 1)] // TOP_K, 0).astype(jnp.int32)
    return dest, row_tok, total.reshape(1), tile_e, first


def _layer_mixers(x, mods, conv_state, c0, n0, m0, lp, dims, precise):
    b, l, d = x.shape
    sh1, sc1, gt1, sh2, sc2, gt2 = mods
    dc, heads, dqk, dv = dims
    bb = b if l < 256 else 1
    tl = _tile(l, 512, SUBLANES)
    act = f32 if precise else bf16
    h, gif = _norm1(x, lp["norm1_g"], sc1, sh1, lp["w_if"], lp["b_if"], bb, tl, precise)
    n_main = 3 * dc + 2 * heads * dqk + 2 * heads * dv
    pm = _proj(h.reshape(b * l, d), lp["w_in"], lp["b_in"], n_main, precise).reshape(b, l, n_main)
    pg = _proj(h.reshape(b * l, d), lp["w_g"], lp["b_g"], 2 * d, precise)
    z, conv_new = _conv_mixer(pm, conv_state, lp["conv_w"], dc, act)
    zm, c1, n1, m1 = _mlstm_mixer(pm, gif, c0, n0, m0, lp["mlstm_norm_g"], 3 * dc, heads, dqk, dv, precise)
    if precise:
        mg = _merge(z.reshape(b * l, dc), zm.reshape(b * l, heads * dv), pg, lp["w_conv_out"], lp["w_mlstm_out"], True)
        mix = _proj(mg, lp["w_out"], jnp.zeros((d,), f32), d, True)
        x1, h2, ti, tg = _post(mix.reshape(b, l, d), x, gt1, lp["norm2_g"], sc2, sh2, lp["w_router"], lp["b_router"],
                               None, bb, tl)
    else:
        mg = _merge(z.reshape(b * l, dc), zm.reshape(b * l, heads * dv), pg, lp["w_conv_out_bf"],
                    lp["w_mlstm_out_bf"], False)
        x1, h2, ti, tg = _post(mg.reshape(b, l, d), x, gt1, lp["norm2_g"], sc2, sh2, lp["w_router"], lp["b_router"],
                               lp["w_out_bf"], bb, tl)
    return x1, h2, ti, tg, conv_new, c1, n1, m1.reshape(b, heads)


def kernel(x_prompt, x_sample, c_prompt, c_sample, state_conv, state_C, state_n, state_m, norm1_g, norm2_g, final_g,
           w_ada, b_ada, w_in, b_in, conv_w, w_conv_out, w_mlstm_out, mlstm_norm_g, w_out, w_router, b_router,
           w_gate_up, b_gate_up, w_down, b_down):
    depth = w_in.shape[0]
    bp, lp_, d = x_prompt.shape
    bs, ls, _ = x_sample.shape
    dc = conv_w.shape[-1]
    kw = conv_w.shape[1]
    heads, dqk, dv = state_C.shape[2:]
    experts = w_router.shape[-1]
    dff = w_down.shape[2]
    assert experts <= LANES and 2 * heads <= LANES
    dims = (dc, heads, dqk, dv)
    off_i = 3 * dc + 2 * heads * dqk + 2 * heads * dv
    off_g = off_i + 2 * heads

    r_all = bp * lp_ + bs * ls
    tm = MOE_ROWS
    n_tiles = -(-(r_all * TOP_K) // tm) + experts

    yp, ys = x_prompt, x_sample
    c_all = jnp.concatenate([c_prompt, c_sample], axis=0)
    c_rows = -(-c_all.shape[0] // SUBLANES) * SUBLANES
    c_all = jnp.pad(c_all, ((0, c_rows - c_all.shape[0]), (0, 0)))
    outs_p = [[], [], [], []]
    outs_s = [[], [], [], []]
    for l in range(depth):
        lp = {
            "norm1_g": norm1_g[l], "norm2_g": norm2_g[l], "w_in": w_in[l], "b_in": b_in[l],
            "w_g": w_in[l][:, off_g:], "b_g": b_in[l][off_g:],
            "w_if": jnp.pad(w_in[l][:, off_i:off_g], ((0, 0), (0, LANES - 2 * heads))),
            "b_if": jnp.pad(b_in[l][off_i:off_g], (0, LANES - 2 * heads)).reshape(1, LANES),
            "conv_w": conv_w[l], "mlstm_norm_g": mlstm_norm_g[l],
            "w_conv_out": w_conv_out[l], "w_mlstm_out": w_mlstm_out[l], "w_out": w_out[l],
            "w_conv_out_bf": w_conv_out[l].astype(bf16), "w_mlstm_out_bf": w_mlstm_out[l].astype(bf16),
            "w_out_bf": w_out[l].astype(bf16),
            "w_router": jnp.pad(w_router[l], ((0, 0), (0, LANES - experts))),
            "b_router": jnp.pad(b_router[l], (0, LANES - experts), constant_values=-jnp.inf).reshape(1, LANES),
        }
        mod = _ada(c_all, w_ada[l], b_ada[l])
        mods_p = [m[:bp].reshape(bp, 1, d) for m in jnp.split(mod, 6, axis=-1)]
        mods_s = [m[bp:bp + bs].reshape(bs, 1, d) for m in jnp.split(mod, 6, axis=-1)]

        zp = (jnp.zeros((bp, kw - 1, dc), f32), jnp.zeros((bp, heads, dqk, dv), f32),
              jnp.zeros((bp, heads, dqk), f32), jnp.zeros((bp, heads), f32))
        x1p, h2p, tip, tgp, cp, Cp, nvp, mvp = _layer_mixers(yp, mods_p, *zp, lp, dims, False)
        x1s, h2s, tis, tgs, cs, Cs, nvs, mvs = _layer_mixers(
            ys, mods_s, state_conv[l], state_C[l], state_n[l], state_m[l], lp, dims, True)

        h2 = jnp.concatenate([h2p.reshape(bp * lp_, d), h2s.reshape(bs * ls, d)], axis=0)
        top_i = jnp.concatenate([tip.reshape(bp * lp_, LANES), tis.reshape(bs * ls, LANES)], axis=0)[:, :TOP_K]
        dest, row_tok, total, tile_e, first = _routing_plan(top_i, experts, tm, n_tiles)
        xs = _dispatch(h2, row_tok, total, tm, n_tiles)
        act = _moe_up(xs, w_gate_up[l], b_gate_up[l], tile_e, first, total, tm, n_tiles, dff)
        yb = _moe_down(act, w_down[l], b_down[l], tile_e, first, total, tm, n_tiles)
        last = l == depth - 1
        yp = _combine(x1p, tgp, mods_p[5], final_g, yb, dest[:bp * lp_ * TOP_K], last)
        ys = _combine(x1s, tgs, mods_s[5], final_g, yb, dest[bp * lp_ * TOP_K:], last)
        for acc, vals in ((outs_p, (cp, Cp, nvp, mvp)), (outs_s, (cs, Cs, nvs, mvs))):
            for a, v in zip(acc, vals):
                a.append(v)
    return (yp, ys, *[jnp.stack(a) for a in outs_p], *[jnp.stack(a) for a in outs_s])
```

```python
import functools

import jax
import jax.numpy as jnp
from jax import lax
from jax.experimental import pallas as pl
from jax.experimental.pallas import tpu as pltpu

EPS = 1e-6
TOP_K = 4
SWIGLU_ALPHA = 1.702
SWIGLU_LIMIT = 7.0

LANES = 128
SUBLANES = 8
VMEM_LIMIT_BYTES = 56 * 1024 * 1024
MOE_ROWS = 512
MLSTM_CHUNK = 256
DMA_UNROLL = 8

f32 = jnp.float32
bf16 = jnp.bfloat16


def _tile(dim, pref, align=LANES):
    if dim <= pref:
        return dim
    t = (pref // align) * align
    while t >= align:
        if dim % t == 0:
            return t
        t -= align
    return dim


def _params(*sem):
    return pltpu.CompilerParams(dimension_semantics=sem, vmem_limit_bytes=VMEM_LIMIT_BYTES)


def _split(a):
    hi = a.astype(bf16)
    return hi, (a - hi.astype(f32)).astype(bf16)


_NN = (((1,), (0,)), ((), ()))
_NT = (((1,), (1,)), ((), ()))
_TN = (((0,), (0,)), ((), ()))


def _dot3(a, b, dims=_NN):
    (ah, al), (bh, bl) = a, b
    d = functools.partial(lax.dot_general, dimension_numbers=dims, preferred_element_type=f32)
    return d(ah, bh) + (d(ah, bl) + d(al, bh))


def _mm(a, b, precise, dims=_NN):
    if precise:
        return _dot3(_split(a), _split(b), dims)
    return lax.dot_general(a.astype(bf16), b.astype(bf16), dims, preferred_element_type=f32)


def _ada_kernel(c_ref, w_ref, b_ref, o_ref):
    c = c_ref[...]
    o_ref[...] = _mm(c * jax.nn.sigmoid(c), w_ref[...], True) + b_ref[...]


def _ada(c, w, b):
    m, d = c.shape
    n = w.shape[1]
    tn = _tile(n, 512)
    return pl.pallas_call(
        _ada_kernel,
        grid=(n // tn,),
        in_specs=[
            pl.BlockSpec((m, d), lambda j: (0, 0)),
            pl.BlockSpec((d, tn), lambda j: (0, j)),
            pl.BlockSpec((1, tn), lambda j: (0, j)),
        ],
        out_specs=pl.BlockSpec((m, tn), lambda j: (0, j)),
        out_shape=jax.ShapeDtypeStruct((m, n), f32),
        compiler_params=_params("arbitrary"),
        name="ada_mod",
    )(c, w, b.reshape(1, n))


def _norm1_kernel(x_ref, g_ref, sc_ref, sh_ref, wif_ref, bif_ref, h_ref, gif_ref, *, precise):
    bb, tl, d = x_ref.shape
    x = x_ref[...]
    y = x * lax.rsqrt(jnp.mean(x * x, axis=-1, keepdims=True) + EPS) * g_ref[...]
    h = y * (1.0 + sc_ref[...]) + sh_ref[...]
    h_ref[...] = h.astype(h_ref.dtype)
    gif = _mm(h.reshape(bb * tl, d), wif_ref[...], precise) + bif_ref[...]
    gif_ref[...] = gif.reshape(bb, tl, LANES)


def _norm1(x, g, sc, sh, wif, bif, bb, tl, precise):
    b, l, d = x.shape
    return pl.pallas_call(
        functools.partial(_norm1_kernel, precise=precise),
        grid=(b // bb, l // tl),
        in_specs=[
            pl.BlockSpec((bb, tl, d), lambda i, j: (i, j, 0)),
            pl.BlockSpec((1, 1, d), lambda i, j: (0, 0, 0)),
            pl.BlockSpec((bb, 1, d), lambda i, j: (i, 0, 0)),
            pl.BlockSpec((bb, 1, d), lambda i, j: (i, 0, 0)),
            pl.BlockSpec((d, LANES), lambda i, j: (0, 0)),
            pl.BlockSpec((1, LANES), lambda i, j: (0, 0)),
        ],
        out_specs=[
            pl.BlockSpec((bb, tl, d), lambda i, j: (i, j, 0)),
            pl.BlockSpec((bb, tl, LANES), lambda i, j: (i, j, 0)),
        ],
        out_shape=[jax.ShapeDtypeStruct((b, l, d), f32 if precise else bf16),
                   jax.ShapeDtypeStruct((b, l, LANES), f32)],
        compiler_params=_params("arbitrary", "arbitrary"),
        name="norm1",
    )(x, g.reshape(1, 1, d), sc, sh, wif, bif)


def _proj_kernel(h_ref, w_ref, b_ref, o_ref, whi_ref, *wlo_ref, precise):
    @pl.when(pl.program_id(1) == 0)
    def _():
        if precise:
            whi_ref[...], wlo_ref[0][...] = _split(w_ref[...])
        else:
            whi_ref[...] = w_ref[...].astype(bf16)

    if precise:
        acc = _dot3(_split(h_ref[...]), (whi_ref[...], wlo_ref[0][...]))
    else:
        acc = jnp.dot(h_ref[...], whi_ref[...], preferred_element_type=f32)
    o_ref[...] = (acc + b_ref[...]).astype(o_ref.dtype)


def _proj(h, w, b, n_cols, precise):
    r, d = h.shape
    tn = _tile(n_cols, 512 if precise else 1024)
    tm = _tile(r, 1024, SUBLANES)
    return pl.pallas_call(
        functools.partial(_proj_kernel, precise=precise),
        grid=(n_cols // tn, r // tm),
        in_specs=[
            pl.BlockSpec((tm, d), lambda j, i: (i, 0)),
            pl.BlockSpec((d, tn), lambda j, i: (0, j)),
            pl.BlockSpec((1, tn), lambda j, i: (0, j)),
        ],
        out_specs=pl.BlockSpec((tm, tn), lambda j, i: (i, j)),
        out_shape=jax.ShapeDtypeStruct((r, n_cols), f32),
        scratch_shapes=[pltpu.VMEM((d, tn), bf16)] * (2 if precise else 1),
        compiler_params=_params("arbitrary", "arbitrary"),
        name="proj",
    )(h, w, b.reshape(1, -1))


def _conv_kernel(bg_ref, cg_ref, xin_ref, st_ref, w_ref, z_ref, ns_ref, carry_ref):
    li = pl.program_id(2)
    tl = bg_ref.shape[1]
    kw = w_ref.shape[0]

    @pl.when(li == 0)
    def _():
        carry_ref[...] = jnp.zeros_like(carry_ref)
        carry_ref[SUBLANES - (kw - 1):, :] = st_ref[0]

    u = cg_ref[0] * xin_ref[0]
    row = lax.broadcasted_iota(jnp.int32, u.shape, 0)
    y = u * w_ref[kw - 1:kw, :]
    for s in range(1, kw):
        us = pltpu.roll(u, s, 0)
        for r in range(s):
            us = jnp.where(row == r, carry_ref[SUBLANES - s + r:SUBLANES - s + r + 1, :], us)
        y = y + us * w_ref[kw - 1 - s:kw - s, :]
    z_ref[0] = (bg_ref[0] * y).astype(z_ref.dtype)
    carry_ref[...] = u[tl - SUBLANES:, :]

    @pl.when(li == pl.num_programs(2) - 1)
    def _():
        ns_ref[0] = u[tl - (kw - 1):, :]


def _conv_mixer(pm, state, conv_w, dc, out_dtype):
    b, l, _ = pm.shape
    kw = conv_w.shape[0]
    tl = _tile(l, 1024, SUBLANES)
    tc = _tile(dc, 512)
    assert tl >= SUBLANES and tl % SUBLANES == 0 and kw - 1 <= SUBLANES and l >= kw - 1
    nc = dc // tc
    return pl.pallas_call(
        _conv_kernel,
        grid=(b, nc, l // tl),
        in_specs=[
            pl.BlockSpec((1, tl, tc), lambda i, c, j: (i, j, c)),
            pl.BlockSpec((1, tl, tc), lambda i, c, j: (i, j, c + nc)),
            pl.BlockSpec((1, tl, tc), lambda i, c, j: (i, j, c + 2 * nc)),
            pl.BlockSpec((1, kw - 1, tc), lambda i, c, j: (i, 0, c)),
            pl.BlockSpec((kw, tc), lambda i, c, j: (0, c)),
        ],
        out_specs=[
            pl.BlockSpec((1, tl, tc), lambda i, c, j: (i, j, c)),
            pl.BlockSpec((1, kw - 1, tc), lambda i, c, j: (i, 0, c)),
        ],
        out_shape=[jax.ShapeDtypeStruct((b, l, dc), out_dtype), jax.ShapeDtypeStruct((b, kw - 1, dc), f32)],
        scratch_shapes=[pltpu.VMEM((SUBLANES, tc), f32)],
        compiler_params=_params("arbitrary", "arbitrary", "arbitrary"),
        name="conv_mixer",
    )(pm, pm, pm, state, conv_w)


def _mlstm_kernel(q_ref, k_ref, v_ref, o_ref, gif_ref, c0_ref, n0_ref, m0_ref, ng_ref,
                  zm_ref, c1_ref, n1_ref, m1_ref, c_s, n_s, m_s, *, heads, dqk, dv, precise):
    ci = pl.program_id(1)
    lc = q_ref.shape[1]

    @pl.when(ci == 0)
    def _():
        c_s[...] = c0_ref[0]
        n_s[...] = n0_ref[0]
        m_s[...] = m0_ref[0]

    gif = gif_ref[0]
    lf = jnp.minimum(gif, 0.0) - jnp.log(1.0 + jnp.exp(-jnp.abs(gif)))
    row = lax.broadcasted_iota(jnp.int32, gif.shape, 0)
    fc = lf
    s = 1
    while s < lc:
        fc = fc + jnp.where(row >= s, pltpu.roll(fc, s, 0), 0.0)
        s *= 2
    gif_t = gif.T
    fc_t = fc.T
    tri = lax.broadcasted_iota(jnp.int32, (lc, lc), 0) >= lax.broadcasted_iota(jnp.int32, (lc, lc), 1)
    scale = dqk ** -0.5

    for h in range(heads):
        ig_col = gif[:, h:h + 1]
        ig_row = gif_t[h:h + 1, :]
        f_col = fc[:, heads + h:heads + h + 1]
        f_row = fc_t[heads + h:heads + h + 1, :]
        m_prev = m_s[:, h:h + 1]
        qf = q_ref[0, :, h * dqk:(h + 1) * dqk]
        kf = k_ref[0, :, h * dqk:(h + 1) * dqk] * scale
        vf = v_ref[0, :, h * dv:(h + 1) * dv]
        c_h = c_s[h]
        n_h = n_s[h:h + 1, :]

        d = jnp.where(tri, f_col - f_row + ig_row, -jnp.inf)
        inter = f_col + m_prev
        m_t = jnp.maximum(inter, jnp.max(d, axis=-1, keepdims=True))
        w_inter = jnp.exp(inter - m_t)
        p = jnp.exp(d - m_t)
        sm = _mm(qf, kf, precise, _NT) * p
        num = w_inter * _mm(qf, c_h, precise) + _mm(sm, vf, precise)
        den = w_inter * jnp.sum(qf * n_h, axis=-1, keepdims=True) + jnp.sum(sm, axis=-1, keepdims=True)
        hh = num / jnp.maximum(jnp.abs(den), jnp.exp(-m_t))

        f_last = f_col[lc - 1:lc, :]
        m_new = m_t[lc - 1:lc, :]
        decay = jnp.exp(f_last + m_prev - m_new)
        w_s = jnp.exp(f_last - f_col + ig_col - m_new)
        kw = kf * w_s
        c_s[h] = decay * c_h + _mm(kw, vf, precise, _TN)
        n_s[h:h + 1, :] = decay * n_h + jnp.sum(kw, axis=0, keepdims=True)
        m_s[:, h:h + 1] = m_new

        hn = hh * lax.rsqrt(jnp.mean(hh * hh, axis=-1, keepdims=True) + EPS) * ng_ref[:, h * dv:(h + 1) * dv]
        zm_ref[0, :, h * dv:(h + 1) * dv] = (hn * jax.nn.sigmoid(o_ref[0, :, h * dv:(h + 1) * dv])).astype(zm_ref.dtype)

    @pl.when(ci == pl.num_programs(1) - 1)
    def _():
        c1_ref[0] = c_s[...]
        n1_ref[0] = n_s[...]
        m1_ref[0] = m_s[...]


def _mlstm_mixer(pm, gif, c0, n0, m0, norm_g, off_q, heads, dqk, dv, precise):
    b, l, _ = pm.shape
    lc = _tile(l, MLSTM_CHUNK, SUBLANES)
    wq, wv = heads * dqk, heads * dv
    off_k, off_v, off_o = off_q + wq, off_q + 2 * wq, off_q + 2 * wq + wv
    assert off_q % wq == 0 and off_v % wv == 0 and off_o % wv == 0
    kern = functools.partial(_mlstm_kernel, heads=heads, dqk=dqk, dv=dv, precise=precise)
    return pl.pallas_call(
        kern,
        grid=(b, l // lc),
        in_specs=[
            pl.BlockSpec((1, lc, wq), lambda i, j: (i, j, off_q // wq)),
            pl.BlockSpec((1, lc, wq), lambda i, j: (i, j, off_k // wq)),
            pl.BlockSpec((1, lc, wv), lambda i, j: (i, j, off_v // wv)),
            pl.BlockSpec((1, lc, wv), lambda i, j: (i, j, off_o // wv)),
            pl.BlockSpec((1, lc, LANES), lambda i, j: (i, j, 0)),
            pl.BlockSpec((1, heads, dqk, dv), lambda i, j: (i, 0, 0, 0)),
            pl.BlockSpec((1, heads, dqk), lambda i, j: (i, 0, 0)),
            pl.BlockSpec((1, 1, heads), lambda i, j: (i, 0, 0)),
            pl.BlockSpec((1, wv), lambda i, j: (0, 0)),
        ],
        out_specs=[
            pl.BlockSpec((1, lc, wv), lambda i, j: (i, j, 0)),
            pl.BlockSpec((1, heads, dqk, dv), lambda i, j: (i, 0, 0, 0)),
            pl.BlockSpec((1, heads, dqk), lambda i, j: (i, 0, 0)),
            pl.BlockSpec((1, 1, heads), lambda i, j: (i, 0, 0)),
        ],
        out_shape=[
            jax.ShapeDtypeStruct((b, l, wv), f32 if precise else bf16),
            jax.ShapeDtypeStruct((b, heads, dqk, dv), f32),
            jax.ShapeDtypeStruct((b, heads, dqk), f32),
            jax.ShapeDtypeStruct((b, 1, heads), f32),
        ],
        scratch_shapes=[pltpu.VMEM((heads, dqk, dv), f32), pltpu.VMEM((heads, dqk), f32), pltpu.VMEM((1, heads), f32)],
        compiler_params=_params("arbitrary", "arbitrary"),
        name="mlstm_mixer",
    )(pm, pm, pm, pm, gif, c0, n0, m0.reshape(b, 1, heads), norm_g.reshape(1, wv))


def _merge_kernel(z_ref, zm_ref, gc_ref, gm_ref, wc_ref, wm_ref, o_ref, *, precise):
    if precise:
        pc = _mm(z_ref[...], wc_ref[...], True)
        pm = _mm(zm_ref[...], wm_ref[...], True)
    else:
        pc = jnp.dot(z_ref[...], wc_ref[...], preferred_element_type=f32)
        pm = jnp.dot(zm_ref[...], wm_ref[...], preferred_element_type=f32)
    o_ref[...] = (jax.nn.sigmoid(gc_ref[...]) * pc + jax.nn.sigmoid(gm_ref[...]) * pm).astype(o_ref.dtype)


def _merge(z, zm, g, wc, wm, precise):
    r, dc = z.shape
    dm = zm.shape[1]
    d = wc.shape[1]
    tn = _tile(d, 512)
    tm = _tile(r, 512, SUBLANES)
    nn = d // tn
    return pl.pallas_call(
        functools.partial(_merge_kernel, precise=precise),
        grid=(nn, r // tm),
        in_specs=[
            pl.BlockSpec((tm, dc), lambda j, i: (i, 0)),
            pl.BlockSpec((tm, dm), lambda j, i: (i, 0)),
            pl.BlockSpec((tm, tn), lambda j, i: (i, j)),
            pl.BlockSpec((tm, tn), lambda j, i: (i, j + nn)),
            pl.BlockSpec((dc, tn), lambda j, i: (0, j)),
            pl.BlockSpec((dm, tn), lambda j, i: (0, j)),
        ],
        out_specs=pl.BlockSpec((tm, tn), lambda j, i: (i, j)),
        out_shape=jax.ShapeDtypeStruct((r, d), f32 if precise else bf16),
        compiler_params=_params("arbitrary", "arbitrary"),
        name="merge",
    )(z, zm, g, g, wc, wm)


def _pack_bf16_pairs(x):
    half = x.shape[1] // 2
    hi = lax.bitcast_convert_type(x[:, :half].astype(bf16).astype(f32), jnp.uint32)
    lo = lax.bitcast_convert_type(x[:, half:].astype(bf16).astype(f32), jnp.uint32)
    return hi | (lo >> 16)


def _unpack_bf16_pairs(u):
    hi = lax.bitcast_convert_type(u & jnp.uint32(0xFFFF0000), f32).astype(bf16)
    lo = lax.bitcast_convert_type(u << 16, f32).astype(bf16)
    return hi, lo


def _post_kernel(mg_ref, x_ref, gt_ref, g_ref, sc_ref, sh_ref, wr_ref, br_ref, cnt0_ref, *rest, has_mix):
    if has_mix:
        x1_ref, h2_ref, plan_ref, tg_ref, cnt_ref, run_ref = rest
    else:
        wo_ref, x1_ref, h2_ref, plan_ref, tg_ref, cnt_ref, run_ref = rest
    bb, tl, d = x_ref.shape
    rows = bb * tl

    @pl.when((pl.program_id(0) == 0) & (pl.program_id(1) == 0))
    def _():
        run_ref[...] = cnt0_ref[...]

    if has_mix:
        mix = mg_ref[...]
    else:
        mix = jnp.dot(mg_ref[...].reshape(rows, d), wo_ref[...], preferred_element_type=f32).reshape(bb, tl, d)
    x1 = x_ref[...] + gt_ref[...] * mix
    x1_ref[...] = x1
    y = x1 * lax.rsqrt(jnp.mean(x1 * x1, axis=-1, keepdims=True) + EPS) * g_ref[...]
    h2 = (y * (1.0 + sc_ref[...]) + sh_ref[...]).reshape(rows, d)
    h2_ref[...] = _pack_bf16_pairs(h2).reshape(bb, tl, d // 2)
    lg = _mm(h2, wr_ref[...], True) + br_ref[...]
    lane = lax.broadcasted_iota(jnp.int32, lg.shape, 1)
    vals, idxs = [], []
    for _ in range(TOP_K):
        mk = jnp.max(lg, axis=-1, keepdims=True)
        ik = jnp.min(jnp.where(lg == mk, lane, LANES), axis=-1, keepdims=True)
        vals.append(mk)
        idxs.append(ik)
        lg = jnp.where(lane == ik, -jnp.inf, lg)
    es = [jnp.exp(v - vals[0]) for v in vals]
    tot = es[0]
    for e in es[1:]:
        tot = tot + e

    sel = [(lane == ik).astype(f32) for ik in idxs]
    picked = sel[0]
    for s in sel[1:]:
        picked = picked + s
    earlier = lax.broadcasted_iota(jnp.int32, (rows, rows), 1) < lax.broadcasted_iota(jnp.int32, (rows, rows), 0)
    before = run_ref[...] + jnp.dot(earlier.astype(bf16), picked.astype(bf16), preferred_element_type=f32)
    run_ref[...] = run_ref[...] + jnp.sum(picked, axis=0, keepdims=True)
    cnt_ref[...] = run_ref[...]

    plan = jnp.zeros(lg.shape, jnp.int32)
    tg = jnp.zeros(lg.shape, f32)
    for k in range(TOP_K):
        rank = jnp.sum(sel[k] * before, axis=-1, keepdims=True).astype(jnp.int32)
        plan = jnp.where(lane == k, idxs[k], plan)
        plan = jnp.where(lane == TOP_K + k, rank, plan)
        tg = jnp.where(lane == k, es[k] / tot, tg)
    plan_ref[...] = plan.reshape(bb, tl, LANES)
    tg_ref[...] = tg.reshape(bb, tl, LANES)


def _post(mg, x, gt, g, sc, sh, wr, br, wo, cnt0, bb, tl):
    b, l, d = x.shape
    row = lambda i, j: (i, j, 0)
    per_b = lambda i, j: (i, 0, 0)
    const2 = lambda i, j: (0, 0)
    in_specs = [
        pl.BlockSpec((bb, tl, d), row),
        pl.BlockSpec((bb, tl, d), row),
        pl.BlockSpec((bb, 1, d), per_b),
        pl.BlockSpec((1, 1, d), lambda i, j: (0, 0, 0)),
        pl.BlockSpec((bb, 1, d), per_b),
        pl.BlockSpec((bb, 1, d), per_b),
        pl.BlockSpec((d, LANES), const2),
        pl.BlockSpec((1, LANES), const2),
        pl.BlockSpec((1, LANES), const2),
    ]
    args = [mg, x, gt, g.reshape(1, 1, d), sc, sh, wr, br, cnt0]
    if wo is not None:
        in_specs.append(pl.BlockSpec((d, d), const2))
        args.append(wo)
    return pl.pallas_call(
        functools.partial(_post_kernel, has_mix=wo is None),
        grid=(b // bb, l // tl),
        in_specs=in_specs,
        out_specs=[
            pl.BlockSpec((bb, tl, d), row),
            pl.BlockSpec((bb, tl, d // 2), row),
            pl.BlockSpec((bb, tl, LANES), row),
            pl.BlockSpec((bb, tl, LANES), row),
            pl.BlockSpec((1, LANES), const2),
        ],
        out_shape=[
            jax.ShapeDtypeStruct((b, l, d), f32),
            jax.ShapeDtypeStruct((b, l, d // 2), jnp.uint32),
            jax.ShapeDtypeStruct((b, l, LANES), jnp.int32),
            jax.ShapeDtypeStruct((b, l, LANES), f32),
            jax.ShapeDtypeStruct((1, LANES), f32),
        ],
        scratch_shapes=[pltpu.VMEM((1, LANES), f32)],
        compiler_params=_params("arbitrary", "arbitrary"),
        name="post_mixer",
    )(*args)


def _start_row_gather(idx_ref, n_rows, src_hbm, dst, sem):
    def issue(i, c):
        pltpu.make_async_copy(src_hbm.at[pl.ds(idx_ref[0, 0, i], 1)], dst.at[pl.ds(i, 1)], sem).start()
        return c

    lax.fori_loop(0, n_rows, issue, 0, unroll=DMA_UNROLL)


def _wait_row_gather(n_rows, src_hbm, dst, sem):
    def drain(i, c):
        pltpu.make_async_copy(src_hbm.at[pl.ds(0, 1)], dst.at[pl.ds(0, 1)], sem).wait()
        return c

    lax.fori_loop(0, n_rows, drain, 0, unroll=DMA_UNROLL)


def _dispatch_kernel(tot_ref, tok_ref, nxt_ref, h2_ref, xs_ref, buf_ref, sem):
    j = pl.program_id(0)
    tm = buf_ref.shape[1]
    slot = j % 2

    @pl.when(j == 0)
    def _():
        _start_row_gather(tok_ref, tm, h2_ref, buf_ref.at[0], sem.at[0])

    @pl.when(j + 1 < tot_ref[0])
    def _():
        _start_row_gather(nxt_ref, tm, h2_ref, buf_ref.at[1 - slot], sem.at[1 - slot])

    @pl.when(j < tot_ref[0])
    def _():
        _wait_row_gather(tm, h2_ref, buf_ref.at[slot], sem.at[slot])
        xs_ref[...] = buf_ref[slot]

    @pl.when(j >= tot_ref[0])
    def _():
        xs_ref[...] = jnp.zeros_like(xs_ref)


def _dispatch(h2, row_tok, total, tm, n_tiles):
    w = h2.shape[1]
    tok3 = row_tok.reshape(n_tiles, 1, tm)
    return pl.pallas_call(
        _dispatch_kernel,
        grid_spec=pltpu.PrefetchScalarGridSpec(
            num_scalar_prefetch=1,
            grid=(n_tiles,),
            in_specs=[
                pl.BlockSpec((1, 1, tm), lambda j, tot: (j, 0, 0), memory_space=pltpu.SMEM),
                pl.BlockSpec((1, 1, tm), lambda j, tot: (jnp.minimum(j + 1, n_tiles - 1), 0, 0),
                             memory_space=pltpu.SMEM),
                pl.BlockSpec(memory_space=pl.ANY),
            ],
            out_specs=pl.BlockSpec((tm, w), lambda j, tot: (j, 0)),
            scratch_shapes=[pltpu.VMEM((2, tm, w), h2.dtype), pltpu.SemaphoreType.DMA((2,))],
        ),
        out_shape=jax.ShapeDtypeStruct((n_tiles * tm, w), h2.dtype),
        compiler_params=_params("arbitrary"),
        name="moe_dispatch",
    )(total, tok3, tok3, h2)


def _moe_up_kernel(te_ref, first_ref, tot_ref, xs_ref, wg_ref, wu_ref, bg_ref, bu_ref, act_ref, wg_s, wu_s):
    j = pl.program_id(1)

    @pl.when(j < tot_ref[0])
    def _():
        @pl.when(first_ref[j] == 1)
        def _():
            wg_s[...] = wg_ref[0].astype(bf16)
            wu_s[...] = wu_ref[0].astype(bf16)

        xa, xb = _unpack_bf16_pairs(xs_ref[...])
        half = xa.shape[1]
        mm = functools.partial(jnp.dot, preferred_element_type=f32)
        gate = mm(xa, wg_s[:half, :]) + mm(xb, wg_s[half:, :]) + bg_ref[0]
        up = mm(xa, wu_s[:half, :]) + mm(xb, wu_s[half:, :]) + bu_ref[0]
        glu = jnp.minimum(gate, SWIGLU_LIMIT)
        lin = jnp.clip(up, -SWIGLU_LIMIT, SWIGLU_LIMIT)
        act_ref[...] = (glu * jax.nn.sigmoid(SWIGLU_ALPHA * glu) * (lin + 1.0)).astype(act_ref.dtype)

    @pl.when(j >= tot_ref[0])
    def _():
        act_ref[...] = jnp.zeros_like(act_ref)


def _moe_up(xs, w_gu, b_gu, tile_e, first, total, tm, n_tiles, dff):
    experts, d, _ = w_gu.shape
    assert xs.shape[1] * 2 == d
    tf = _tile(dff, 512)
    nf = dff // tf
    last = lambda j, tot: jnp.minimum(j, tot[0] - 1)
    return pl.pallas_call(
        _moe_up_kernel,
        grid_spec=pltpu.PrefetchScalarGridSpec(
            num_scalar_prefetch=3,
            grid=(nf, n_tiles),
            in_specs=[
                pl.BlockSpec((tm, d // 2), lambda f, j, te, fi, tot: (last(j, tot), 0)),
                pl.BlockSpec((1, d, tf), lambda f, j, te, fi, tot: (te[j], 0, f)),
                pl.BlockSpec((1, d, tf), lambda f, j, te, fi, tot: (te[j], 0, f + nf)),
                pl.BlockSpec((1, 1, tf), lambda f, j, te, fi, tot: (te[j], 0, f)),
                pl.BlockSpec((1, 1, tf), lambda f, j, te, fi, tot: (te[j], 0, f + nf)),
            ],
            out_specs=pl.BlockSpec((tm, tf), lambda f, j, te, fi, tot: (j, f)),
            scratch_shapes=[pltpu.VMEM((d, tf), bf16), pltpu.VMEM((d, tf), bf16)],
        ),
        out_shape=jax.ShapeDtypeStruct((n_tiles * tm, dff), bf16),
        compiler_params=_params("arbitrary", "arbitrary"),
        name="moe_up",
    )(tile_e, first, total, xs, w_gu, w_gu, b_gu.reshape(experts, 1, -1), b_gu.reshape(experts, 1, -1))


def _moe_down_kernel(te_ref, first_ref, tot_ref, act_ref, w_ref, b_ref, y_ref, w_s):
    j = pl.program_id(1)

    @pl.when(j < tot_ref[0])
    def _():
        @pl.when(first_ref[j] == 1)
        def _():
            w_s[...] = w_ref[0].astype(bf16)

        y_ref[...] = jnp.dot(act_ref[...], w_s[...], preferred_element_type=f32) + b_ref[0]

    @pl.when(j >= tot_ref[0])
    def _():
        y_ref[...] = jnp.zeros_like(y_ref)


def _moe_down(act, w_dn, b_dn, tile_e, first, total, tm, n_tiles):
    dff = act.shape[1]
    experts, _, d = w_dn.shape
    tn = _tile(d, 1024)
    last = lambda j, tot: jnp.minimum(j, tot[0] - 1)
    return pl.pallas_call(
        _moe_down_kernel,
        grid_spec=pltpu.PrefetchScalarGridSpec(
            num_scalar_prefetch=3,
            grid=(d // tn, n_tiles),
            in_specs=[
                pl.BlockSpec((tm, dff), lambda n, j, te, fi, tot: (last(j, tot), 0)),
                pl.BlockSpec((1, dff, tn), lambda n, j, te, fi, tot: (te[j], 0, n)),
                pl.BlockSpec((1, 1, tn), lambda n, j, te, fi, tot: (te[j], 0, n)),
            ],
            out_specs=pl.BlockSpec((tm, tn), lambda n, j, te, fi, tot: (j, n)),
            scratch_shapes=[pltpu.VMEM((dff, tn), bf16)],
        ),
        out_shape=jax.ShapeDtypeStruct((n_tiles * tm, d), f32),
        compiler_params=_params("arbitrary", "arbitrary"),
        name="moe_down",
    )(tile_e, first, total, act, w_dn, b_dn.reshape(experts, 1, d))


def _combine_kernel(dest_ref, nxt_ref, x1_ref, tg_ref, gt_ref, fg_ref, yb_ref, o_ref, buf_ref, sem, *, final_norm):
    tl = x1_ref.shape[1]
    n = TOP_K * tl
    step = pl.program_id(0) * pl.num_programs(1) + pl.program_id(1)
    n_steps = pl.num_programs(0) * pl.num_programs(1)
    slot = step % 2

    @pl.when(step == 0)
    def _():
        _start_row_gather(dest_ref, n, yb_ref, buf_ref.at[0], sem.at[0])

    @pl.when(step + 1 < n_steps)
    def _():
        _start_row_gather(nxt_ref, n, yb_ref, buf_ref.at[1 - slot], sem.at[1 - slot])

    _wait_row_gather(n, yb_ref, buf_ref.at[slot], sem.at[slot])
    tg = tg_ref[0]
    moe = tg[:, 0:1] * buf_ref[slot, 0:tl, :]
    for k in range(1, TOP_K):
        moe = moe + tg[:, k:k + 1] * buf_ref[slot, k * tl:(k + 1) * tl, :]
    xo = x1_ref[0] + gt_ref[0] * moe
    if final_norm:
        xo = xo * lax.rsqrt(jnp.mean(xo * xo, axis=-1, keepdims=True) + EPS) * fg_ref[...]
    o_ref[0] = xo


def _combine(x1, tg, gt, fg, yb, dest, final_norm):
    b, l, d = x1.shape
    tl = _tile(l, 128, SUBLANES)
    nl = l // tl
    dest_t = dest.reshape(b * nl, tl, TOP_K).transpose(0, 2, 1).reshape(b * nl, 1, TOP_K * tl)
    return pl.pallas_call(
        functools.partial(_combine_kernel, final_norm=final_norm),
        grid=(b, nl),
        in_specs=[
            pl.BlockSpec((1, 1, TOP_K * tl), lambda i, j: (i * nl + j, 0, 0), memory_space=pltpu.SMEM),
            pl.BlockSpec((1, 1, TOP_K * tl), lambda i, j: (jnp.minimum(i * nl + j + 1, b * nl - 1), 0, 0),
                         memory_space=pltpu.SMEM),
            pl.BlockSpec((1, tl, d), lambda i, j: (i, j, 0)),
            pl.BlockSpec((1, tl, LANES), lambda i, j: (i, j, 0)),
            pl.BlockSpec((1, 1, d), lambda i, j: (i, 0, 0)),
            pl.BlockSpec((1, d), lambda i, j: (0, 0)),
            pl.BlockSpec(memory_space=pl.ANY),
        ],
        out_specs=pl.BlockSpec((1, tl, d), lambda i, j: (i, j, 0)),
        out_shape=jax.ShapeDtypeStruct((b, l, d), f32),
        scratch_shapes=[pltpu.VMEM((2, TOP_K * tl, d), f32), pltpu.SemaphoreType.DMA((2,))],
        compiler_params=_params("arbitrary", "arbitrary"),
        name="moe_combine",
    )(dest_t, dest_t, x1, tg, gt, fg.reshape(1, d), yb)


def _lookup(table, idx):
    hit = idx[..., None] == jnp.arange(table.shape[0], dtype=jnp.int32)
    return jnp.sum(jnp.where(hit, table, 0), axis=-1)


def _count_le(ends, x):
    return jnp.sum((ends <= x[..., None]).astype(jnp.int32), axis=-1)


def _routing_plan(plan, counts, experts, tm, n_tiles):
    exp_id, rank = plan[:, :TOP_K], plan[:, TOP_K:]
    na = exp_id.size
    padded = ((counts + tm - 1) // tm) * tm
    pends = jnp.cumsum(padded)
    pstart = pends - padded
    total = pends[-1] // tm
    dest = (_lookup(pstart, exp_id) + rank).reshape(-1)
    tile_id = jnp.minimum(jnp.arange(n_tiles, dtype=jnp.int32), total - 1)
    tile_e = jnp.minimum(_count_le(pends, tile_id * tm), experts - 1)
    first = jnp.concatenate([jnp.ones((1,), jnp.int32), (tile_e[1:] != tile_e[:-1]).astype(jnp.int32)])
    n_pad = n_tiles * tm - na
    gaps = padded - counts
    gap_ends = jnp.cumsum(gaps)
    p = jnp.arange(n_pad, dtype=jnp.int32)
    pe = jnp.minimum(_count_le(gap_ends, p), experts - 1)
    in_gap = p < gap_ends[-1]
    pad_pos = jnp.where(in_gap, _lookup(pstart + counts - (gap_ends - gaps), pe) + p, pends[-1] + (p - gap_ends[-1]))
    tok = jnp.arange(na, dtype=jnp.int32) // TOP_K
    _, row_tok = lax.sort((jnp.concatenate([dest, pad_pos]), jnp.concatenate([tok, jnp.zeros((n_pad,), jnp.int32)])),
                          num_keys=1)
    return dest, row_tok, total.reshape(1), tile_e, first


def _layer_mixers(x, mods, conv_state, c0, n0, m0, cnt0, lp, dims, precise):
    b, l, d = x.shape
    sh1, sc1, gt1, sh2, sc2, gt2 = mods
    dc, heads, dqk, dv = dims
    bb = b if l < 256 else 1
    tl = _tile(l, 512, SUBLANES)
    act = f32 if precise else bf16
    h, gif = _norm1(x, lp["norm1_g"], sc1, sh1, lp["w_if"], lp["b_if"], bb, tl, precise)
    n_main = 3 * dc + 2 * heads * dqk + 2 * heads * dv
    pm = _proj(h.reshape(b * l, d), lp["w_in"], lp["b_in"], n_main, precise).reshape(b, l, n_main)
    pg = _proj(h.reshape(b * l, d), lp["w_g"], lp["b_g"], 2 * d, precise)
    z, conv_new = _conv_mixer(pm, conv_state, lp["conv_w"], dc, act)
    zm, c1, n1, m1 = _mlstm_mixer(pm, gif, c0, n0, m0, lp["mlstm_norm_g"], 3 * dc, heads, dqk, dv, precise)
    if precise:
        mg = _merge(z.reshape(b * l, dc), zm.reshape(b * l, heads * dv), pg, lp["w_conv_out"], lp["w_mlstm_out"], True)
        mix = _proj(mg, lp["w_out"], jnp.zeros((d,), f32), d, True)
        post = _post(mix.reshape(b, l, d), x, gt1, lp["norm2_g"], sc2, sh2, lp["w_router"], lp["b_router"],
                     None, cnt0, bb, tl)
    else:
        mg = _merge(z.reshape(b * l, dc), zm.reshape(b * l, heads * dv), pg, lp["w_conv_out_bf"],
                    lp["w_mlstm_out_bf"], False)
        post = _post(mg.reshape(b, l, d), x, gt1, lp["norm2_g"], sc2, sh2, lp["w_router"], lp["b_router"],
                     lp["w_out_bf"], cnt0, bb, tl)
    return (*post, conv_new, c1, n1, m1.reshape(b, heads))


def kernel(x_prompt, x_sample, c_prompt, c_sample, state_conv, state_C, state_n, state_m, norm1_g, norm2_g, final_g,
           w_ada, b_ada, w_in, b_in, conv_w, w_conv_out, w_mlstm_out, mlstm_norm_g, w_out, w_router, b_router,
           w_gate_up, b_gate_up, w_down, b_down):
    depth = w_in.shape[0]
    bp, lp_, d = x_prompt.shape
    bs, ls, _ = x_sample.shape
    dc = conv_w.shape[-1]
    kw = conv_w.shape[1]
    heads, dqk, dv = state_C.shape[2:]
    experts = w_router.shape[-1]
    dff = w_down.shape[2]
    assert experts <= LANES and 2 * heads <= LANES
    dims = (dc, heads, dqk, dv)
    off_i = 3 * dc + 2 * heads * dqk + 2 * heads * dv
    off_g = off_i + 2 * heads

    r_all = bp * lp_ + bs * ls
    tm = MOE_ROWS
    n_tiles = -(-(r_all * TOP_K) // tm) + experts

    yp, ys = x_prompt, x_sample
    c_all = jnp.concatenate([c_prompt, c_sample], axis=0)
    c_rows = -(-c_all.shape[0] // SUBLANES) * SUBLANES
    c_all = jnp.pad(c_all, ((0, c_rows - c_all.shape[0]), (0, 0)))
    outs_p = [[], [], [], []]
    outs_s = [[], [], [], []]
    for l in range(depth):
        lp = {
            "norm1_g": norm1_g[l], "norm2_g": norm2_g[l], "w_in": w_in[l], "b_in": b_in[l],
            "w_g": w_in[l][:, off_g:], "b_g": b_in[l][off_g:],
            "w_if": jnp.pad(w_in[l][:, off_i:off_g], ((0, 0), (0, LANES - 2 * heads))),
            "b_if": jnp.pad(b_in[l][off_i:off_g], (0, LANES - 2 * heads)).reshape(1, LANES),
            "conv_w": conv_w[l], "mlstm_norm_g": mlstm_norm_g[l],
            "w_conv_out": w_conv_out[l], "w_mlstm_out": w_mlstm_out[l], "w_out": w_out[l],
            "w_conv_out_bf": w_conv_out[l].astype(bf16), "w_mlstm_out_bf": w_mlstm_out[l].astype(bf16),
            "w_out_bf": w_out[l].astype(bf16),
            "w_router": jnp.pad(w_router[l], ((0, 0), (0, LANES - experts))),
            "b_router": jnp.pad(b_router[l], (0, LANES - experts), constant_values=-jnp.inf).reshape(1, LANES),
        }
        mod = _ada(c_all, w_ada[l], b_ada[l])
        mods_p = [m[:bp].reshape(bp, 1, d) for m in jnp.split(mod, 6, axis=-1)]
        mods_s = [m[bp:bp + bs].reshape(bs, 1, d) for m in jnp.split(mod, 6, axis=-1)]

        zp = (jnp.zeros((bp, kw - 1, dc), f32), jnp.zeros((bp, heads, dqk, dv), f32),
              jnp.zeros((bp, heads, dqk), f32), jnp.zeros((bp, heads), f32))
        x1p, h2p, plp, tgp, cntp, cp, Cp, nvp, mvp = _layer_mixers(
            yp, mods_p, *zp, jnp.zeros((1, LANES), f32), lp, dims, False)
        x1s, h2s, pls, tgs, cnt, cs, Cs, nvs, mvs = _layer_mixers(
            ys, mods_s, state_conv[l], state_C[l], state_n[l], state_m[l], cntp, lp, dims, True)

        h2 = jnp.concatenate([h2p.reshape(bp * lp_, d // 2), h2s.reshape(bs * ls, d // 2)], axis=0)
        plan = jnp.concatenate([plp.reshape(bp * lp_, LANES), pls.reshape(bs * ls, LANES)], axis=0)[:, :2 * TOP_K]
        counts = cnt[0, :experts].astype(jnp.int32)
        dest, row_tok, total, tile_e, first = _routing_plan(plan, counts, experts, tm, n_tiles)
        xs = _dispatch(h2, row_tok, total, tm, n_tiles)
        act = _moe_up(xs, w_gate_up[l], b_gate_up[l], tile_e, first, total, tm, n_tiles, dff)
        yb = _moe_down(act, w_down[l], b_down[l], tile_e, first, total, tm, n_tiles)
        last = l == depth - 1
        yp = _combine(x1p, tgp, mods_p[5], final_g, yb, dest[:bp * lp_ * TOP_K], last)
        ys = _combine(x1s, tgs, mods_s[5], final_g, yb, dest[bp * lp_ * TOP_K:], last)
        for acc, vals in ((outs_p, (cp, Cp, nvp, mvp)), (outs_s, (cs, Cs, nvs, mvs))):
            for a, v in zip(acc, vals):
                a.append(v)
    return (yp, ys, *[jnp.stack(a) for a in outs_p], *[jnp.stack(a) for a in outs_s])
```

```python
import functools

import jax
import jax.numpy as jnp
from jax import lax
from jax.experimental import pallas as pl
from jax.experimental.pallas import tpu as pltpu

EPS = 1e-6
TOP_K = 4
SWIGLU_ALPHA = 1.702
SWIGLU_LIMIT = 7.0

LANES = 128
SUBLANES = 8
VMEM_LIMIT_BYTES = 56 * 1024 * 1024
MOE_ROWS = 512
MLSTM_CHUNK = 256
DMA_UNROLL = 8

f32 = jnp.float32
bf16 = jnp.bfloat16


def _tile(dim, pref, align=LANES):
    if dim <= pref:
        return dim
    t = (pref // align) * align
    while t >= align:
        if dim % t == 0:
            return t
        t -= align
    return dim


def _params(*sem):
    return pltpu.CompilerParams(dimension_semantics=sem, vmem_limit_bytes=VMEM_LIMIT_BYTES)


def _split(a):
    hi = a.astype(bf16)
    return hi, (a - hi.astype(f32)).astype(bf16)


_NN = (((1,), (0,)), ((), ()))
_NT = (((1,), (1,)), ((), ()))
_TN = (((0,), (0,)), ((), ()))


def _dot3(a, b, dims=_NN):
    (ah, al), (bh, bl) = a, b
    d = functools.partial(lax.dot_general, dimension_numbers=dims, preferred_element_type=f32)
    return d(ah, bh) + (d(ah, bl) + d(al, bh))


def _mm(a, b, precise, dims=_NN):
    if precise:
        return _dot3(_split(a), _split(b), dims)
    return lax.dot_general(a.astype(bf16), b.astype(bf16), dims, preferred_element_type=f32)


def _ada_kernel(c_ref, w_ref, b_ref, o_ref):
    c = c_ref[...]
    o_ref[...] = _mm(c * jax.nn.sigmoid(c), w_ref[...], True) + b_ref[...]


def _ada(c, w, b):
    m, d = c.shape
    n = w.shape[1]
    tn = _tile(n, 512)
    return pl.pallas_call(
        _ada_kernel,
        grid=(n // tn,),
        in_specs=[
            pl.BlockSpec((m, d), lambda j: (0, 0)),
            pl.BlockSpec((d, tn), lambda j: (0, j)),
            pl.BlockSpec((1, tn), lambda j: (0, j)),
        ],
        out_specs=pl.BlockSpec((m, tn), lambda j: (0, j)),
        out_shape=jax.ShapeDtypeStruct((m, n), f32),
        compiler_params=_params("arbitrary"),
        name="ada_mod",
    )(c, w, b.reshape(1, n))


def _norm1_kernel(x_ref, g_ref, sc_ref, sh_ref, wif_ref, bif_ref, h_ref, gif_ref, *, precise):
    bb, tl, d = x_ref.shape
    x = x_ref[...]
    y = x * lax.rsqrt(jnp.mean(x * x, axis=-1, keepdims=True) + EPS) * g_ref[...]
    h = y * (1.0 + sc_ref[...]) + sh_ref[...]
    h_ref[...] = h.astype(h_ref.dtype)
    gif = _mm(h.reshape(bb * tl, d), wif_ref[...], precise) + bif_ref[...]
    gif_ref[...] = gif.reshape(bb, tl, LANES)


def _norm1(x, g, sc, sh, wif, bif, bb, tl, precise):
    b, l, d = x.shape
    return pl.pallas_call(
        functools.partial(_norm1_kernel, precise=precise),
        grid=(b // bb, l // tl),
        in_specs=[
            pl.BlockSpec((bb, tl, d), lambda i, j: (i, j, 0)),
            pl.BlockSpec((1, 1, d), lambda i, j: (0, 0, 0)),
            pl.BlockSpec((bb, 1, d), lambda i, j: (i, 0, 0)),
            pl.BlockSpec((bb, 1, d), lambda i, j: (i, 0, 0)),
            pl.BlockSpec((d, LANES), lambda i, j: (0, 0)),
            pl.BlockSpec((1, LANES), lambda i, j: (0, 0)),
        ],
        out_specs=[
            pl.BlockSpec((bb, tl, d), lambda i, j: (i, j, 0)),
            pl.BlockSpec((bb, tl, LANES), lambda i, j: (i, j, 0)),
        ],
        out_shape=[jax.ShapeDtypeStruct((b, l, d), f32 if precise else bf16),
                   jax.ShapeDtypeStruct((b, l, LANES), f32)],
        compiler_params=_params("arbitrary", "arbitrary"),
        name="norm1",
    )(x, g.reshape(1, 1, d), sc, sh, wif, bif)


def _proj_kernel(h_ref, w_ref, b_ref, o_ref, whi_ref, *wlo_ref, precise):
    @pl.when(pl.program_id(1) == 0)
    def _():
        if precise:
            whi_ref[...], wlo_ref[0][...] = _split(w_ref[...])
        else:
            whi_ref[...] = w_ref[...].astype(bf16)

    if precise:
        acc = _dot3(_split(h_ref[...]), (whi_ref[...], wlo_ref[0][...]))
    else:
        acc = jnp.dot(h_ref[...], whi_ref[...], preferred_element_type=f32)
    o_ref[...] = (acc + b_ref[...]).astype(o_ref.dtype)


def _proj(h, w, b, n_cols, precise):
    r, d = h.shape
    tn = _tile(n_cols, 512 if precise else 1024)
    tm = _tile(r, 1024, SUBLANES)
    return pl.pallas_call(
        functools.partial(_proj_kernel, precise=precise),
        grid=(n_cols // tn, r // tm),
        in_specs=[
            pl.BlockSpec((tm, d), lambda j, i: (i, 0)),
            pl.BlockSpec((d, tn), lambda j, i: (0, j)),
            pl.BlockSpec((1, tn), lambda j, i: (0, j)),
        ],
        out_specs=pl.BlockSpec((tm, tn), lambda j, i: (i, j)),
        out_shape=jax.ShapeDtypeStruct((r, n_cols), f32),
        scratch_shapes=[pltpu.VMEM((d, tn), bf16)] * (2 if precise else 1),
        compiler_params=_params("arbitrary", "arbitrary"),
        name="proj",
    )(h, w, b.reshape(1, -1))


def _conv_kernel(bg_ref, cg_ref, xin_ref, st_ref, w_ref, z_ref, ns_ref, carry_ref):
    li = pl.program_id(2)
    tl = bg_ref.shape[1]
    kw = w_ref.shape[0]

    @pl.when(li == 0)
    def _():
        carry_ref[...] = jnp.zeros_like(carry_ref)
        carry_ref[SUBLANES - (kw - 1):, :] = st_ref[0]

    u = cg_ref[0] * xin_ref[0]
    row = lax.broadcasted_iota(jnp.int32, u.shape, 0)
    y = u * w_ref[kw - 1:kw, :]
    for s in range(1, kw):
        us = pltpu.roll(u, s, 0)
        for r in range(s):
            us = jnp.where(row == r, carry_ref[SUBLANES - s + r:SUBLANES - s + r + 1, :], us)
        y = y + us * w_ref[kw - 1 - s:kw - s, :]
    z_ref[0] = (bg_ref[0] * y).astype(z_ref.dtype)
    carry_ref[...] = u[tl - SUBLANES:, :]

    @pl.when(li == pl.num_programs(2) - 1)
    def _():
        ns_ref[0] = u[tl - (kw - 1):, :]


def _conv_mixer(pm, state, conv_w, dc, out_dtype):
    b, l, _ = pm.shape
    kw = conv_w.shape[0]
    tl = _tile(l, 1024, SUBLANES)
    tc = _tile(dc, 512)
    assert tl >= SUBLANES and tl % SUBLANES == 0 and kw - 1 <= SUBLANES and l >= kw - 1
    nc = dc // tc
    return pl.pallas_call(
        _conv_kernel,
        grid=(b, nc, l // tl),
        in_specs=[
            pl.BlockSpec((1, tl, tc), lambda i, c, j: (i, j, c)),
            pl.BlockSpec((1, tl, tc), lambda i, c, j: (i, j, c + nc)),
            pl.BlockSpec((1, tl, tc), lambda i, c, j: (i, j, c + 2 * nc)),
            pl.BlockSpec((1, kw - 1, tc), lambda i, c, j: (i, 0, c)),
            pl.BlockSpec((kw, tc), lambda i, c, j: (0, c)),
        ],
        out_specs=[
            pl.BlockSpec((1, tl, tc), lambda i, c, j: (i, j, c)),
            pl.BlockSpec((1, kw - 1, tc), lambda i, c, j: (i, 0, c)),
        ],
        out_shape=[jax.ShapeDtypeStruct((b, l, dc), out_dtype), jax.ShapeDtypeStruct((b, kw - 1, dc), f32)],
        scratch_shapes=[pltpu.VMEM((SUBLANES, tc), f32)],
        compiler_params=_params("arbitrary", "arbitrary", "arbitrary"),
        name="conv_mixer",
    )(pm, pm, pm, state, conv_w)


def _mlstm_kernel(q_ref, k_ref, v_ref, o_ref, gif_ref, c0_ref, n0_ref, m0_ref, ng_ref,
                  zm_ref, c1_ref, n1_ref, m1_ref, c_s, n_s, m_s, *, heads, dqk, dv, precise):
    ci = pl.program_id(1)
    lc = q_ref.shape[1]

    @pl.when(ci == 0)
    def _():
        c_s[...] = c0_ref[0]
        n_s[...] = n0_ref[0]
        m_s[...] = m0_ref[0]

    gif = gif_ref[0]
    lf = jnp.minimum(gif, 0.0) - jnp.log(1.0 + jnp.exp(-jnp.abs(gif)))
    row = lax.broadcasted_iota(jnp.int32, gif.shape, 0)
    fc = lf
    s = 1
    while s < lc:
        fc = fc + jnp.where(row >= s, pltpu.roll(fc, s, 0), 0.0)
        s *= 2
    gif_t = gif.T
    fc_t = fc.T
    tri = lax.broadcasted_iota(jnp.int32, (lc, lc), 0) >= lax.broadcasted_iota(jnp.int32, (lc, lc), 1)
    scale = dqk ** -0.5

    for h in range(heads):
        ig_col = gif[:, h:h + 1]
        ig_row = gif_t[h:h + 1, :]
        f_col = fc[:, heads + h:heads + h + 1]
        f_row = fc_t[heads + h:heads + h + 1, :]
        m_prev = m_s[:, h:h + 1]
        qf = q_ref[0, :, h * dqk:(h + 1) * dqk]
        kf = k_ref[0, :, h * dqk:(h + 1) * dqk] * scale
        vf = v_ref[0, :, h * dv:(h + 1) * dv]
        c_h = c_s[h]
        n_h = n_s[h:h + 1, :]

        d = jnp.where(tri, f_col - f_row + ig_row, -jnp.inf)
        inter = f_col + m_prev
        m_t = jnp.maximum(inter, jnp.max(d, axis=-1, keepdims=True))
        w_inter = jnp.exp(inter - m_t)
        p = jnp.exp(d - m_t)
        sm = _mm(qf, kf, precise, _NT) * p
        num = w_inter * _mm(qf, c_h, precise) + _mm(sm, vf, precise)
        den = w_inter * jnp.sum(qf * n_h, axis=-1, keepdims=True) + jnp.sum(sm, axis=-1, keepdims=True)
        hh = num / jnp.maximum(jnp.abs(den), jnp.exp(-m_t))

        f_last = f_col[lc - 1:lc, :]
        m_new = m_t[lc - 1:lc, :]
        decay = jnp.exp(f_last + m_prev - m_new)
        w_s = jnp.exp(f_last - f_col + ig_col - m_new)
        kw = kf * w_s
        c_s[h] = decay * c_h + _mm(kw, vf, precise, _TN)
        n_s[h:h + 1, :] = decay * n_h + jnp.sum(kw, axis=0, keepdims=True)
        m_s[:, h:h + 1] = m_new

        hn = hh * lax.rsqrt(jnp.mean(hh * hh, axis=-1, keepdims=True) + EPS) * ng_ref[:, h * dv:(h + 1) * dv]
        zm_ref[0, :, h * dv:(h + 1) * dv] = (hn * jax.nn.sigmoid(o_ref[0, :, h * dv:(h + 1) * dv])).astype(zm_ref.dtype)

    @pl.when(ci == pl.num_programs(1) - 1)
    def _():
        c1_ref[0] = c_s[...]
        n1_ref[0] = n_s[...]
        m1_ref[0] = m_s[...]


def _mlstm_mixer(pm, gif, c0, n0, m0, norm_g, off_q, heads, dqk, dv, precise):
    b, l, _ = pm.shape
    lc = _tile(l, MLSTM_CHUNK, SUBLANES)
    wq, wv = heads * dqk, heads * dv
    off_k, off_v, off_o = off_q + wq, off_q + 2 * wq, off_q + 2 * wq + wv
    assert off_q % wq == 0 and off_v % wv == 0 and off_o % wv == 0
    kern = functools.partial(_mlstm_kernel, heads=heads, dqk=dqk, dv=dv, precise=precise)
    return pl.pallas_call(
        kern,
        grid=(b, l // lc),
        in_specs=[
            pl.BlockSpec((1, lc, wq), lambda i, j: (i, j, off_q // wq)),
            pl.BlockSpec((1, lc, wq), lambda i, j: (i, j, off_k // wq)),
            pl.BlockSpec((1, lc, wv), lambda i, j: (i, j, off_v // wv)),
            pl.BlockSpec((1, lc, wv), lambda i, j: (i, j, off_o // wv)),
            pl.BlockSpec((1, lc, LANES), lambda i, j: (i, j, 0)),
            pl.BlockSpec((1, heads, dqk, dv), lambda i, j: (i, 0, 0, 0)),
            pl.BlockSpec((1, heads, dqk), lambda i, j: (i, 0, 0)),
            pl.BlockSpec((1, 1, heads), lambda i, j: (i, 0, 0)),
            pl.BlockSpec((1, wv), lambda i, j: (0, 0)),
        ],
        out_specs=[
            pl.BlockSpec((1, lc, wv), lambda i, j: (i, j, 0)),
            pl.BlockSpec((1, heads, dqk, dv), lambda i, j: (i, 0, 0, 0)),
            pl.BlockSpec((1, heads, dqk), lambda i, j: (i, 0, 0)),
            pl.BlockSpec((1, 1, heads), lambda i, j: (i, 0, 0)),
        ],
        out_shape=[
            jax.ShapeDtypeStruct((b, l, wv), f32 if precise else bf16),
            jax.ShapeDtypeStruct((b, heads, dqk, dv), f32),
            jax.ShapeDtypeStruct((b, heads, dqk), f32),
            jax.ShapeDtypeStruct((b, 1, heads), f32),
        ],
        scratch_shapes=[pltpu.VMEM((heads, dqk, dv), f32), pltpu.VMEM((heads, dqk), f32), pltpu.VMEM((1, heads), f32)],
        compiler_params=_params("arbitrary", "arbitrary"),
        name="mlstm_mixer",
    )(pm, pm, pm, pm, gif, c0, n0, m0.reshape(b, 1, heads), norm_g.reshape(1, wv))


def _merge_kernel(z_ref, zm_ref, gc_ref, gm_ref, wc_ref, wm_ref, o_ref, *, precise):
    if precise:
        pc = _mm(z_ref[...], wc_ref[...], True)
        pm = _mm(zm_ref[...], wm_ref[...], True)
    else:
        pc = jnp.dot(z_ref[...], wc_ref[...], preferred_element_type=f32)
        pm = jnp.dot(zm_ref[...], wm_ref[...], preferred_element_type=f32)
    o_ref[...] = (jax.nn.sigmoid(gc_ref[...]) * pc + jax.nn.sigmoid(gm_ref[...]) * pm).astype(o_ref.dtype)


def _merge(z, zm, g, wc, wm, precise):
    r, dc = z.shape
    dm = zm.shape[1]
    d = wc.shape[1]
    tn = _tile(d, 512)
    tm = _tile(r, 1024, SUBLANES)
    nn = d // tn
    return pl.pallas_call(
        functools.partial(_merge_kernel, precise=precise),
        grid=(nn, r // tm),
        in_specs=[
            pl.BlockSpec((tm, dc), lambda j, i: (i, 0)),
            pl.BlockSpec((tm, dm), lambda j, i: (i, 0)),
            pl.BlockSpec((tm, tn), lambda j, i: (i, j)),
            pl.BlockSpec((tm, tn), lambda j, i: (i, j + nn)),
            pl.BlockSpec((dc, tn), lambda j, i: (0, j)),
            pl.BlockSpec((dm, tn), lambda j, i: (0, j)),
        ],
        out_specs=pl.BlockSpec((tm, tn), lambda j, i: (i, j)),
        out_shape=jax.ShapeDtypeStruct((r, d), f32 if precise else bf16),
        compiler_params=_params("arbitrary", "arbitrary"),
        name="merge",
    )(z, zm, g, g, wc, wm)


def _pack_bf16_pairs(x):
    half = x.shape[1] // 2
    hi = lax.bitcast_convert_type(x[:, :half].astype(bf16).astype(f32), jnp.uint32)
    lo = lax.bitcast_convert_type(x[:, half:].astype(bf16).astype(f32), jnp.uint32)
    return hi | (lo >> 16)


def _unpack_bf16_pairs(u):
    hi = lax.bitcast_convert_type(u & jnp.uint32(0xFFFF0000), f32).astype(bf16)
    lo = lax.bitcast_convert_type(u << 16, f32).astype(bf16)
    return hi, lo


def _post_kernel(mg_ref, x_ref, gt_ref, g_ref, sc_ref, sh_ref, wr_ref, br_ref, cnt0_ref, *rest, has_mix):
    if has_mix:
        x1_ref, h2_ref, plan_ref, tg_ref, cnt_ref, run_ref = rest
    else:
        wo_ref, x1_ref, h2_ref, plan_ref, tg_ref, cnt_ref, run_ref = rest
    bb, tl, d = x_ref.shape
    rows = bb * tl

    @pl.when((pl.program_id(0) == 0) & (pl.program_id(1) == 0))
    def _():
        run_ref[...] = cnt0_ref[...]

    if has_mix:
        mix = mg_ref[...]
    else:
        mix = jnp.dot(mg_ref[...].reshape(rows, d), wo_ref[...], preferred_element_type=f32).reshape(bb, tl, d)
    x1 = x_ref[...] + gt_ref[...] * mix
    x1_ref[...] = x1
    y = x1 * lax.rsqrt(jnp.mean(x1 * x1, axis=-1, keepdims=True) + EPS) * g_ref[...]
    h2 = (y * (1.0 + sc_ref[...]) + sh_ref[...]).reshape(rows, d)
    h2_ref[...] = _pack_bf16_pairs(h2).reshape(bb, tl, d // 2)
    lg = _mm(h2, wr_ref[...], True) + br_ref[...]
    lane = lax.broadcasted_iota(jnp.int32, lg.shape, 1)
    vals, idxs = [], []
    for _ in range(TOP_K):
        mk = jnp.max(lg, axis=-1, keepdims=True)
        ik = jnp.min(jnp.where(lg == mk, lane, LANES), axis=-1, keepdims=True)
        vals.append(mk)
        idxs.append(ik)
        lg = jnp.where(lane == ik, -jnp.inf, lg)
    es = [jnp.exp(v - vals[0]) for v in vals]
    tot = es[0]
    for e in es[1:]:
        tot = tot + e

    sel = [(lane == ik).astype(f32) for ik in idxs]
    picked = sel[0]
    for s in sel[1:]:
        picked = picked + s
    earlier = lax.broadcasted_iota(jnp.int32, (rows, rows), 1) < lax.broadcasted_iota(jnp.int32, (rows, rows), 0)
    before = run_ref[...] + jnp.dot(earlier.astype(bf16), picked.astype(bf16), preferred_element_type=f32)
    run_ref[...] = run_ref[...] + jnp.sum(picked, axis=0, keepdims=True)
    cnt_ref[...] = run_ref[...]

    plan = jnp.zeros(lg.shape, jnp.int32)
    tg = jnp.zeros(lg.shape, f32)
    for k in range(TOP_K):
        rank = jnp.sum(sel[k] * before, axis=-1, keepdims=True).astype(jnp.int32)
        plan = jnp.where(lane == k, idxs[k], plan)
        plan = jnp.where(lane == TOP_K + k, rank, plan)
        tg = jnp.where(lane == k, es[k] / tot, tg)
    plan_ref[...] = plan.reshape(bb, tl, LANES)
    tg_ref[...] = tg.reshape(bb, tl, LANES)


def _post(mg, x, gt, g, sc, sh, wr, br, wo, cnt0, bb, tl):
    b, l, d = x.shape
    row = lambda i, j: (i, j, 0)
    per_b = lambda i, j: (i, 0, 0)
    const2 = lambda i, j: (0, 0)
    in_specs = [
        pl.BlockSpec((bb, tl, d), row),
        pl.BlockSpec((bb, tl, d), row),
        pl.BlockSpec((bb, 1, d), per_b),
        pl.BlockSpec((1, 1, d), lambda i, j: (0, 0, 0)),
        pl.BlockSpec((bb, 1, d), per_b),
        pl.BlockSpec((bb, 1, d), per_b),
        pl.BlockSpec((d, LANES), const2),
        pl.BlockSpec((1, LANES), const2),
        pl.BlockSpec((1, LANES), const2),
    ]
    args = [mg, x, gt, g.reshape(1, 1, d), sc, sh, wr, br, cnt0]
    if wo is not None:
        in_specs.append(pl.BlockSpec((d, d), const2))
        args.append(wo)
    return pl.pallas_call(
        functools.partial(_post_kernel, has_mix=wo is None),
        grid=(b // bb, l // tl),
        in_specs=in_specs,
        out_specs=[
            pl.BlockSpec((bb, tl, d), row),
            pl.BlockSpec((bb, tl, d // 2), row),
            pl.BlockSpec((bb, tl, LANES), row),
            pl.BlockSpec((bb, tl, LANES), row),
            pl.BlockSpec((1, LANES), const2),
        ],
        out_shape=[
            jax.ShapeDtypeStruct((b, l, d), f32),
            jax.ShapeDtypeStruct((b, l, d // 2), jnp.uint32),
            jax.ShapeDtypeStruct((b, l, LANES), jnp.int32),
            jax.ShapeDtypeStruct((b, l, LANES), f32),
            jax.ShapeDtypeStruct((1, LANES), f32),
        ],
        scratch_shapes=[pltpu.VMEM((1, LANES), f32)],
        compiler_params=_params("arbitrary", "arbitrary"),
        name="post_mixer",
    )(*args)


def _start_row_gather(idx_ref, n_rows, src_hbm, dst, sem):
    def issue(i, c):
        pltpu.make_async_copy(src_hbm.at[pl.ds(idx_ref[0, 0, i], 1)], dst.at[pl.ds(i, 1)], sem).start()
        return c

    lax.fori_loop(0, n_rows, issue, 0, unroll=DMA_UNROLL)


def _wait_row_gather(n_rows, src_hbm, dst, sem):
    def drain(i, c):
        pltpu.make_async_copy(src_hbm.at[pl.ds(0, 1)], dst.at[pl.ds(0, 1)], sem).wait()
        return c

    lax.fori_loop(0, n_rows, drain, 0, unroll=DMA_UNROLL)


def _dispatch_kernel(tot_ref, tok_ref, nxt_ref, h2_ref, xs_ref, buf_ref, sem):
    j = pl.program_id(0)
    tm = buf_ref.shape[1]
    slot = j % 2

    @pl.when(j == 0)
    def _():
        _start_row_gather(tok_ref, tm, h2_ref, buf_ref.at[0], sem.at[0])

    @pl.when(j + 1 < tot_ref[0])
    def _():
        _start_row_gather(nxt_ref, tm, h2_ref, buf_ref.at[1 - slot], sem.at[1 - slot])

    @pl.when(j < tot_ref[0])
    def _():
        _wait_row_gather(tm, h2_ref, buf_ref.at[slot], sem.at[slot])
        xs_ref[...] = buf_ref[slot]

    @pl.when(j >= tot_ref[0])
    def _():
        xs_ref[...] = jnp.zeros_like(xs_ref)


def _dispatch(h2, row_tok, total, tm, n_tiles):
    w = h2.shape[1]
    tok3 = row_tok.reshape(n_tiles, 1, tm)
    return pl.pallas_call(
        _dispatch_kernel,
        grid_spec=pltpu.PrefetchScalarGridSpec(
            num_scalar_prefetch=1,
            grid=(n_tiles,),
            in_specs=[
                pl.BlockSpec((1, 1, tm), lambda j, tot: (j, 0, 0), memory_space=pltpu.SMEM),
                pl.BlockSpec((1, 1, tm), lambda j, tot: (jnp.minimum(j + 1, n_tiles - 1), 0, 0),
                             memory_space=pltpu.SMEM),
                pl.BlockSpec(memory_space=pl.ANY),
            ],
            out_specs=pl.BlockSpec((tm, w), lambda j, tot: (j, 0)),
            scratch_shapes=[pltpu.VMEM((2, tm, w), h2.dtype), pltpu.SemaphoreType.DMA((2,))],
        ),
        out_shape=jax.ShapeDtypeStruct((n_tiles * tm, w), h2.dtype),
        compiler_params=_params("arbitrary"),
        name="moe_dispatch",
    )(total, tok3, tok3, h2)


def _expert_weight_ring(first_ref, ta_ref, nact_ref, n_outer, copies, convert):
    o, j = pl.program_id(0), pl.program_id(1)

    @pl.when(first_ref[j] == 1)
    def _():
        a = ta_ref[j]
        n_act = nact_ref[0]
        q = o * n_act + a
        slot = q % 2

        @pl.when(q == 0)
        def _():
            for c in copies(o, a, slot):
                c.start()

        wrap = a + 1 >= n_act
        o2 = jnp.where(wrap, o + 1, o)
        a2 = jnp.where(wrap, 0, a + 1)

        @pl.when(o2 < n_outer)
        def _():
            for c in copies(o2, a2, 1 - slot):
                c.start()

        for c in copies(o, a, slot):
            c.wait()
        convert(slot)


def _moe_up_kernel(te_ref, first_ref, tot_ref, ta_ref, swe_ref, nact_ref, xs_ref, bg_ref, bu_ref, wgu_hbm,
                   act_ref, wst, wg_s, wu_s, sem, *, nf, dff):
    j = pl.program_id(1)
    tf = wg_s.shape[1]

    def copies(f, a, slot):
        e = swe_ref[a]
        col = pl.multiple_of(f * tf, tf)
        return (pltpu.make_async_copy(wgu_hbm.at[e, :, pl.ds(col, tf)], wst.at[slot, 0], sem.at[slot, 0]),
                pltpu.make_async_copy(wgu_hbm.at[e, :, pl.ds(pl.multiple_of(dff + col, tf), tf)], wst.at[slot, 1],
                                      sem.at[slot, 1]))

    def convert(slot):
        wg_s[...] = wst[slot, 0].astype(bf16)
        wu_s[...] = wst[slot, 1].astype(bf16)

    @pl.when(j < tot_ref[0])
    def _():
        _expert_weight_ring(first_ref, ta_ref, nact_ref, nf, copies, convert)
        xa, xb = _unpack_bf16_pairs(xs_ref[...])
        half = xa.shape[1]
        mm = functools.partial(jnp.dot, preferred_element_type=f32)
        gate = mm(xa, wg_s[:half, :]) + mm(xb, wg_s[half:, :]) + bg_ref[0]
        up = mm(xa, wu_s[:half, :]) + mm(xb, wu_s[half:, :]) + bu_ref[0]
        glu = jnp.minimum(gate, SWIGLU_LIMIT)
        lin = jnp.clip(up, -SWIGLU_LIMIT, SWIGLU_LIMIT)
        act_ref[...] = (glu * jax.nn.sigmoid(SWIGLU_ALPHA * glu) * (lin + 1.0)).astype(act_ref.dtype)

    @pl.when(j >= tot_ref[0])
    def _():
        act_ref[...] = jnp.zeros_like(act_ref)


def _moe_up(xs, w_gu, b_gu, sched, tm, n_tiles, dff):
    experts, d, _ = w_gu.shape
    assert xs.shape[1] * 2 == d
    tf = _tile(dff, 512)
    nf = dff // tf
    last = lambda j, tot: jnp.minimum(j, tot[0] - 1)
    return pl.pallas_call(
        functools.partial(_moe_up_kernel, nf=nf, dff=dff),
        grid_spec=pltpu.PrefetchScalarGridSpec(
            num_scalar_prefetch=6,
            grid=(nf, n_tiles),
            in_specs=[
                pl.BlockSpec((tm, d // 2), lambda f, j, te, fi, tot, *_: (last(j, tot), 0)),
                pl.BlockSpec((1, 1, tf), lambda f, j, te, *_: (te[j], 0, f)),
                pl.BlockSpec((1, 1, tf), lambda f, j, te, *_: (te[j], 0, f + nf)),
                pl.BlockSpec(memory_space=pl.ANY),
            ],
            out_specs=pl.BlockSpec((tm, tf), lambda f, j, *_: (j, f)),
            scratch_shapes=[pltpu.VMEM((2, 2, d, tf), f32), pltpu.VMEM((d, tf), bf16), pltpu.VMEM((d, tf), bf16),
                            pltpu.SemaphoreType.DMA((2, 2))],
        ),
        out_shape=jax.ShapeDtypeStruct((n_tiles * tm, dff), bf16),
        compiler_params=_params("arbitrary", "arbitrary"),
        name="moe_up",
    )(*sched, xs, b_gu.reshape(experts, 1, -1), b_gu.reshape(experts, 1, -1), w_gu)


def _moe_down_kernel(te_ref, first_ref, tot_ref, ta_ref, swe_ref, nact_ref, act_ref, b_ref, wdn_hbm,
                     y_ref, wst, w_s, sem, *, nn):
    j = pl.program_id(1)
    tn = w_s.shape[1]

    def copies(n, a, slot):
        col = pl.multiple_of(n * tn, tn)
        return (pltpu.make_async_copy(wdn_hbm.at[swe_ref[a], :, pl.ds(col, tn)], wst.at[slot], sem.at[slot]),)

    def convert(slot):
        w_s[...] = wst[slot].astype(bf16)

    @pl.when(j < tot_ref[0])
    def _():
        _expert_weight_ring(first_ref, ta_ref, nact_ref, nn, copies, convert)
        y_ref[...] = jnp.dot(act_ref[...], w_s[...], preferred_element_type=f32) + b_ref[0]

    @pl.when(j >= tot_ref[0])
    def _():
        y_ref[...] = jnp.zeros_like(y_ref)


def _moe_down(act, w_dn, b_dn, sched, tm, n_tiles):
    dff = act.shape[1]
    experts, _, d = w_dn.shape
    tn = _tile(d, 1024)
    nn = d // tn
    last = lambda j, tot: jnp.minimum(j, tot[0] - 1)
    return pl.pallas_call(
        functools.partial(_moe_down_kernel, nn=nn),
        grid_spec=pltpu.PrefetchScalarGridSpec(
            num_scalar_prefetch=6,
            grid=(nn, n_tiles),
            in_specs=[
                pl.BlockSpec((tm, dff), lambda n, j, te, fi, tot, *_: (last(j, tot), 0)),
                pl.BlockSpec((1, 1, tn), lambda n, j, te, *_: (te[j], 0, n)),
                pl.BlockSpec(memory_space=pl.ANY),
            ],
            out_specs=pl.BlockSpec((tm, tn), lambda n, j, *_: (j, n)),
            scratch_shapes=[pltpu.VMEM((2, dff, tn), f32), pltpu.VMEM((dff, tn), bf16), pltpu.SemaphoreType.DMA((2,))],
        ),
        out_shape=jax.ShapeDtypeStruct((n_tiles * tm, d), f32),
        compiler_params=_params("arbitrary", "arbitrary"),
        name="moe_down",
    )(*sched, act, b_dn.reshape(experts, 1, d), w_dn)


def _combine_kernel(dest_ref, nxt_ref, x1_ref, tg_ref, gt_ref, fg_ref, yb_ref, o_ref, buf_ref, sem, *, final_norm):
    tl = x1_ref.shape[1]
    n = TOP_K * tl
    step = pl.program_id(0) * pl.num_programs(1) + pl.program_id(1)
    n_steps = pl.num_programs(0) * pl.num_programs(1)
    slot = step % 2

    @pl.when(step == 0)
    def _():
        _start_row_gather(dest_ref, n, yb_ref, buf_ref.at[0], sem.at[0])

    @pl.when(step + 1 < n_steps)
    def _():
        _start_row_gather(nxt_ref, n, yb_ref, buf_ref.at[1 - slot], sem.at[1 - slot])

    _wait_row_gather(n, yb_ref, buf_ref.at[slot], sem.at[slot])
    tg = tg_ref[0]
    moe = tg[:, 0:1] * buf_ref[slot, 0:tl, :]
    for k in range(1, TOP_K):
        moe = moe + tg[:, k:k + 1] * buf_ref[slot, k * tl:(k + 1) * tl, :]
    xo = x1_ref[0] + gt_ref[0] * moe
    if final_norm:
        xo = xo * lax.rsqrt(jnp.mean(xo * xo, axis=-1, keepdims=True) + EPS) * fg_ref[...]
    o_ref[0] = xo


def _combine(x1, tg, gt, fg, yb, dest, final_norm):
    b, l, d = x1.shape
    tl = _tile(l, 128, SUBLANES)
    nl = l // tl
    dest_t = dest.reshape(b * nl, tl, TOP_K).transpose(0, 2, 1).reshape(b * nl, 1, TOP_K * tl)
    return pl.pallas_call(
        functools.partial(_combine_kernel, final_norm=final_norm),
        grid=(b, nl),
        in_specs=[
            pl.BlockSpec((1, 1, TOP_K * tl), lambda i, j: (i * nl + j, 0, 0), memory_space=pltpu.SMEM),
            pl.BlockSpec((1, 1, TOP_K * tl), lambda i, j: (jnp.minimum(i * nl + j + 1, b * nl - 1), 0, 0),
                         memory_space=pltpu.SMEM),
            pl.BlockSpec((1, tl, d), lambda i, j: (i, j, 0)),
            pl.BlockSpec((1, tl, LANES), lambda i, j: (i, j, 0)),
            pl.BlockSpec((1, 1, d), lambda i, j: (i, 0, 0)),
            pl.BlockSpec((1, d), lambda i, j: (0, 0)),
            pl.BlockSpec(memory_space=pl.ANY),
        ],
        out_specs=pl.BlockSpec((1, tl, d), lambda i, j: (i, j, 0)),
        out_shape=jax.ShapeDtypeStruct((b, l, d), f32),
        scratch_shapes=[pltpu.VMEM((2, TOP_K * tl, d), f32), pltpu.SemaphoreType.DMA((2,))],
        compiler_params=_params("arbitrary", "arbitrary"),
        name="moe_combine",
    )(dest_t, dest_t, x1, tg, gt, fg.reshape(1, d), yb)


def _lookup(table, idx):
    hit = idx[..., None] == jnp.arange(table.shape[0], dtype=jnp.int32)
    return jnp.sum(jnp.where(hit, table, 0), axis=-1)


def _count_le(ends, x):
    return jnp.sum((ends <= x[..., None]).astype(jnp.int32), axis=-1)


def _routing_plan(plan, counts, experts, tm, n_tiles):
    exp_id, rank = plan[:, :TOP_K], plan[:, TOP_K:]
    na = exp_id.size
    padded = ((counts + tm - 1) // tm) * tm
    pends = jnp.cumsum(padded)
    pstart = pends - padded
    total = pends[-1] // tm
    dest = (_lookup(pstart, exp_id) + rank).reshape(-1)
    tile_id = jnp.minimum(jnp.arange(n_tiles, dtype=jnp.int32), total - 1)
    tile_e = jnp.minimum(_count_le(pends, tile_id * tm), experts - 1)
    first = jnp.concatenate([jnp.ones((1,), jnp.int32), (tile_e[1:] != tile_e[:-1]).astype(jnp.int32)])
    active = (counts > 0).astype(jnp.int32)
    act_idx = jnp.cumsum(active) - 1
    n_active = jnp.sum(active).reshape(1)
    slots = jnp.arange(experts, dtype=jnp.int32)
    active_e = jnp.sum(jnp.where((act_idx[None, :] == slots[:, None]) & (active[None, :] == 1), slots[None, :], 0), axis=1)
    sched = (tile_e, first, total.reshape(1), _lookup(act_idx, tile_e), active_e, n_active)
    n_pad = n_tiles * tm - na
    gaps = padded - counts
    gap_ends = jnp.cumsum(gaps)
    p = jnp.arange(n_pad, dtype=jnp.int32)
    pe = jnp.minimum(_count_le(gap_ends, p), experts - 1)
    in_gap = p < gap_ends[-1]
    pad_pos = jnp.where(in_gap, _lookup(pstart + counts - (gap_ends - gaps), pe) + p, pends[-1] + (p - gap_ends[-1]))
    tok = jnp.arange(na, dtype=jnp.int32) // TOP_K
    _, row_tok = lax.sort((jnp.concatenate([dest, pad_pos]), jnp.concatenate([tok, jnp.zeros((n_pad,), jnp.int32)])),
                          num_keys=1)
    return dest, row_tok, sched


def _layer_mixers(x, mods, conv_state, c0, n0, m0, cnt0, lp, dims, precise):
    b, l, d = x.shape
    sh1, sc1, gt1, sh2, sc2, gt2 = mods
    dc, heads, dqk, dv = dims
    bb = b if l < 256 else 1
    tl = _tile(l, 512, SUBLANES)
    act = f32 if precise else bf16
    h, gif = _norm1(x, lp["norm1_g"], sc1, sh1, lp["w_if"], lp["b_if"], bb, tl, precise)
    n_main = 3 * dc + 2 * heads * dqk + 2 * heads * dv
    pm = _proj(h.reshape(b * l, d), lp["w_in"], lp["b_in"], n_main, precise).reshape(b, l, n_main)
    pg = _proj(h.reshape(b * l, d), lp["w_g"], lp["b_g"], 2 * d, precise)
    z, conv_new = _conv_mixer(pm, conv_state, lp["conv_w"], dc, act)
    zm, c1, n1, m1 = _mlstm_mixer(pm, gif, c0, n0, m0, lp["mlstm_norm_g"], 3 * dc, heads, dqk, dv, precise)
    if precise:
        mg = _merge(z.reshape(b * l, dc), zm.reshape(b * l, heads * dv), pg, lp["w_conv_out"], lp["w_mlstm_out"], True)
        mix = _proj(mg, lp["w_out"], jnp.zeros((d,), f32), d, True)
        post = _post(mix.reshape(b, l, d), x, gt1, lp["norm2_g"], sc2, sh2, lp["w_router"], lp["b_router"],
                     None, cnt0, bb, tl)
    else:
        mg = _merge(z.reshape(b * l, dc), zm.reshape(b * l, heads * dv), pg, lp["w_conv_out_bf"],
                    lp["w_mlstm_out_bf"], False)
        post = _post(mg.reshape(b, l, d), x, gt1, lp["norm2_g"], sc2, sh2, lp["w_router"], lp["b_router"],
                     lp["w_out_bf"], cnt0, bb, tl)
    return (*post, conv_new, c1, n1, m1.reshape(b, heads))


def kernel(x_prompt, x_sample, c_prompt, c_sample, state_conv, state_C, state_n, state_m, norm1_g, norm2_g, final_g,
           w_ada, b_ada, w_in, b_in, conv_w, w_conv_out, w_mlstm_out, mlstm_norm_g, w_out, w_router, b_router,
           w_gate_up, b_gate_up, w_down, b_down):
    depth = w_in.shape[0]
    bp, lp_, d = x_prompt.shape
    bs, ls, _ = x_sample.shape
    dc = conv_w.shape[-1]
    kw = conv_w.shape[1]
    heads, dqk, dv = state_C.shape[2:]
    experts = w_router.shape[-1]
    dff = w_down.shape[2]
    assert experts <= LANES and 2 * heads <= LANES
    dims = (dc, heads, dqk, dv)
    off_i = 3 * dc + 2 * heads * dqk + 2 * heads * dv
    off_g = off_i + 2 * heads

    r_all = bp * lp_ + bs * ls
    tm = MOE_ROWS
    n_tiles = -(-(r_all * TOP_K) // tm) + experts

    yp, ys = x_prompt, x_sample
    c_all = jnp.concatenate([c_prompt, c_sample], axis=0)
    c_rows = -(-c_all.shape[0] // SUBLANES) * SUBLANES
    c_all = jnp.pad(c_all, ((0, c_rows - c_all.shape[0]), (0, 0)))
    outs_p = [[], [], [], []]
    outs_s = [[], [], [], []]
    for l in range(depth):
        lp = {
            "norm1_g": norm1_g[l], "norm2_g": norm2_g[l], "w_in": w_in[l], "b_in": b_in[l],
            "w_g": w_in[l][:, off_g:], "b_g": b_in[l][off_g:],
            "w_if": jnp.pad(w_in[l][:, off_i:off_g], ((0, 0), (0, LANES - 2 * heads))),
            "b_if": jnp.pad(b_in[l][off_i:off_g], (0, LANES - 2 * heads)).reshape(1, LANES),
            "conv_w": conv_w[l], "mlstm_norm_g": mlstm_norm_g[l],
            "w_conv_out": w_conv_out[l], "w_mlstm_out": w_mlstm_out[l], "w_out": w_out[l],
            "w_conv_out_bf": w_conv_out[l].astype(bf16), "w_mlstm_out_bf": w_mlstm_out[l].astype(bf16),
            "w_out_bf": w_out[l].astype(bf16),
            "w_router": jnp.pad(w_router[l], ((0, 0), (0, LANES - experts))),
            "b_router": jnp.pad(b_router[l], (0, LANES - experts), constant_values=-jnp.inf).reshape(1, LANES),
        }
        mod = _ada(c_all, w_ada[l], b_ada[l])
        mods_p = [m[:bp].reshape(bp, 1, d) for m in jnp.split(mod, 6, axis=-1)]
        mods_s = [m[bp:bp + bs].reshape(bs, 1, d) for m in jnp.split(mod, 6, axis=-1)]

        zp = (jnp.zeros((bp, kw - 1, dc), f32), jnp.zeros((bp, heads, dqk, dv), f32),
              jnp.zeros((bp, heads, dqk), f32), jnp.zeros((bp, heads), f32))
        x1p, h2p, plp, tgp, cntp, cp, Cp, nvp, mvp = _layer_mixers(
            yp, mods_p, *zp, jnp.zeros((1, LANES), f32), lp, dims, False)
        x1s, h2s, pls, tgs, cnt, cs, Cs, nvs, mvs = _layer_mixers(
            ys, mods_s, state_conv[l], state_C[l], state_n[l], state_m[l], cntp, lp, dims, True)

        h2 = jnp.concatenate([h2p.reshape(bp * lp_, d // 2), h2s.reshape(bs * ls, d // 2)], axis=0)
        plan = jnp.concatenate([plp.reshape(bp * lp_, LANES), pls.reshape(bs * ls, LANES)], axis=0)[:, :2 * TOP_K]
        counts = cnt[0, :experts].astype(jnp.int32)
        dest, row_tok, sched = _routing_plan(plan, counts, experts, tm, n_tiles)
        xs = _dispatch(h2, row_tok, sched[2], tm, n_tiles)
        act = _moe_up(xs, w_gate_up[l], b_gate_up[l], sched, tm, n_tiles, dff)
        yb = _moe_down(act, w_down[l], b_down[l], sched, tm, n_tiles)
        last = l == depth - 1
        yp = _combine(x1p, tgp, mods_p[5], final_g, yb, dest[:bp * lp_ * TOP_K], last)
        ys = _combine(x1s, tgs, mods_s[5], final_g, yb, dest[bp * lp_ * TOP_K:], last)
        for acc, vals in ((outs_p, (cp, Cp, nvp, mvp)), (outs_s, (cs, Cs, nvs, mvs))):
            for a, v in zip(acc, vals):
                a.append(v)
    return (yp, ys, *[jnp.stack(a) for a in outs_p], *[jnp.stack(a) for a in outs_s])
```

```python
import functools

import jax
import jax.numpy as jnp
from jax import lax
from jax.experimental import pallas as pl
from jax.experimental.pallas import tpu as pltpu

EPS = 1e-6
TOP_K = 4
SWIGLU_ALPHA = 1.702
SWIGLU_LIMIT = 7.0

LANES = 128
SUBLANES = 8
VMEM_LIMIT_BYTES = 56 * 1024 * 1024
MOE_ROWS = 512
MLSTM_CHUNK = 256
DMA_UNROLL = 8

f32 = jnp.float32
bf16 = jnp.bfloat16


def _tile(dim, pref, align=LANES):
    if dim <= pref:
        return dim
    t = (pref // align) * align
    while t >= align:
        if dim % t == 0:
            return t
        t -= align
    return dim


def _params(*sem):
    return pltpu.CompilerParams(dimension_semantics=sem, vmem_limit_bytes=VMEM_LIMIT_BYTES)


def _split(a):
    hi = a.astype(bf16)
    return hi, (a - hi.astype(f32)).astype(bf16)


_NN = (((1,), (0,)), ((), ()))
_NT = (((1,), (1,)), ((), ()))
_TN = (((0,), (0,)), ((), ()))


def _dot3(a, b, dims=_NN):
    (ah, al), (bh, bl) = a, b
    d = functools.partial(lax.dot_general, dimension_numbers=dims, preferred_element_type=f32)
    return d(ah, bh) + (d(ah, bl) + d(al, bh))


def _mm(a, b, precise, dims=_NN):
    if precise:
        return _dot3(_split(a), _split(b), dims)
    return lax.dot_general(a.astype(bf16), b.astype(bf16), dims, preferred_element_type=f32)


def _ada_kernel(c_ref, w_ref, b_ref, o_ref):
    c = c_ref[...]
    o_ref[...] = _mm(c * jax.nn.sigmoid(c), w_ref[...], True) + b_ref[...]


def _ada(c, w, b):
    m, d = c.shape
    n = w.shape[1]
    tn = _tile(n, 512)
    return pl.pallas_call(
        _ada_kernel,
        grid=(n // tn,),
        in_specs=[
            pl.BlockSpec((m, d), lambda j: (0, 0)),
            pl.BlockSpec((d, tn), lambda j: (0, j)),
            pl.BlockSpec((1, tn), lambda j: (0, j)),
        ],
        out_specs=pl.BlockSpec((m, tn), lambda j: (0, j)),
        out_shape=jax.ShapeDtypeStruct((m, n), f32),
        compiler_params=_params("arbitrary"),
        name="ada_mod",
    )(c, w, b.reshape(1, n))


def _norm1_kernel(x_ref, g_ref, sc_ref, sh_ref, wif_ref, bif_ref, h_ref, gif_ref, *, precise):
    bb, tl, d = x_ref.shape
    x = x_ref[...]
    y = x * lax.rsqrt(jnp.mean(x * x, axis=-1, keepdims=True) + EPS) * g_ref[...]
    h = y * (1.0 + sc_ref[...]) + sh_ref[...]
    h_ref[...] = h.astype(h_ref.dtype)
    gif = _mm(h.reshape(bb * tl, d), wif_ref[...], precise) + bif_ref[...]
    gif_ref[...] = gif.reshape(bb, tl, LANES)


def _norm1(x, g, sc, sh, wif, bif, bb, tl, precise):
    b, l, d = x.shape
    return pl.pallas_call(
        functools.partial(_norm1_kernel, precise=precise),
        grid=(b // bb, l // tl),
        in_specs=[
            pl.BlockSpec((bb, tl, d), lambda i, j: (i, j, 0)),
            pl.BlockSpec((1, 1, d), lambda i, j: (0, 0, 0)),
            pl.BlockSpec((bb, 1, d), lambda i, j: (i, 0, 0)),
            pl.BlockSpec((bb, 1, d), lambda i, j: (i, 0, 0)),
            pl.BlockSpec((d, LANES), lambda i, j: (0, 0)),
            pl.BlockSpec((1, LANES), lambda i, j: (0, 0)),
        ],
        out_specs=[
            pl.BlockSpec((bb, tl, d), lambda i, j: (i, j, 0)),
            pl.BlockSpec((bb, tl, LANES), lambda i, j: (i, j, 0)),
        ],
        out_shape=[jax.ShapeDtypeStruct((b, l, d), f32 if precise else bf16),
                   jax.ShapeDtypeStruct((b, l, LANES), f32)],
        compiler_params=_params("arbitrary", "arbitrary"),
        name="norm1",
    )(x, g.reshape(1, 1, d), sc, sh, wif, bif)


def _proj_kernel(h_ref, w_ref, b_ref, o_ref, whi_ref, *wlo_ref, precise):
    @pl.when(pl.program_id(1) == 0)
    def _():
        if precise:
            whi_ref[...], wlo_ref[0][...] = _split(w_ref[...])
        else:
            whi_ref[...] = w_ref[...].astype(bf16)

    if precise:
        acc = _dot3(_split(h_ref[...]), (whi_ref[...], wlo_ref[0][...]))
    else:
        acc = jnp.dot(h_ref[...], whi_ref[...], preferred_element_type=f32)
    o_ref[...] = (acc + b_ref[...]).astype(o_ref.dtype)


def _proj(h, w, b, n_cols, precise):
    r, d = h.shape
    tn = _tile(n_cols, 512 if precise else 1024)
    tm = _tile(r, 1024, SUBLANES)
    return pl.pallas_call(
        functools.partial(_proj_kernel, precise=precise),
        grid=(n_cols // tn, r // tm),
        in_specs=[
            pl.BlockSpec((tm, d), lambda j, i: (i, 0)),
            pl.BlockSpec((d, tn), lambda j, i: (0, j)),
            pl.BlockSpec((1, tn), lambda j, i: (0, j)),
        ],
        out_specs=pl.BlockSpec((tm, tn), lambda j, i: (i, j)),
        out_shape=jax.ShapeDtypeStruct((r, n_cols), f32),
        scratch_shapes=[pltpu.VMEM((d, tn), bf16)] * (2 if precise else 1),
        compiler_params=_params("arbitrary", "arbitrary"),
        name="proj",
    )(h, w, b.reshape(1, -1))


def _conv_kernel(bg_ref, cg_ref, xin_ref, st_ref, w_ref, z_ref, ns_ref, carry_ref):
    li = pl.program_id(2)
    tl = bg_ref.shape[1]
    kw = w_ref.shape[0]

    @pl.when(li == 0)
    def _():
        carry_ref[...] = jnp.zeros_like(carry_ref)
        carry_ref[SUBLANES - (kw - 1):, :] = st_ref[0]

    u = cg_ref[0] * xin_ref[0]
    row = lax.broadcasted_iota(jnp.int32, u.shape, 0)
    y = u * w_ref[kw - 1:kw, :]
    for s in range(1, kw):
        us = pltpu.roll(u, s, 0)
        for r in range(s):
            us = jnp.where(row == r, carry_ref[SUBLANES - s + r:SUBLANES - s + r + 1, :], us)
        y = y + us * w_ref[kw - 1 - s:kw - s, :]
    z_ref[0] = (bg_ref[0] * y).astype(z_ref.dtype)
    carry_ref[...] = u[tl - SUBLANES:, :]

    @pl.when(li == pl.num_programs(2) - 1)
    def _():
        ns_ref[0] = u[tl - (kw - 1):, :]


def _conv_mixer(pm, state, conv_w, dc, out_dtype):
    b, l, _ = pm.shape
    kw = conv_w.shape[0]
    tl = _tile(l, 1024, SUBLANES)
    tc = _tile(dc, 512)
    assert tl >= SUBLANES and tl % SUBLANES == 0 and kw - 1 <= SUBLANES and l >= kw - 1
    nc = dc // tc
    return pl.pallas_call(
        _conv_kernel,
        grid=(b, nc, l // tl),
        in_specs=[
            pl.BlockSpec((1, tl, tc), lambda i, c, j: (i, j, c)),
            pl.BlockSpec((1, tl, tc), lambda i, c, j: (i, j, c + nc)),
            pl.BlockSpec((1, tl, tc), lambda i, c, j: (i, j, c + 2 * nc)),
            pl.BlockSpec((1, kw - 1, tc), lambda i, c, j: (i, 0, c)),
            pl.BlockSpec((kw, tc), lambda i, c, j: (0, c)),
        ],
        out_specs=[
            pl.BlockSpec((1, tl, tc), lambda i, c, j: (i, j, c)),
            pl.BlockSpec((1, kw - 1, tc), lambda i, c, j: (i, 0, c)),
        ],
        out_shape=[jax.ShapeDtypeStruct((b, l, dc), out_dtype), jax.ShapeDtypeStruct((b, kw - 1, dc), f32)],
        scratch_shapes=[pltpu.VMEM((SUBLANES, tc), f32)],
        compiler_params=_params("arbitrary", "arbitrary", "arbitrary"),
        name="conv_mixer",
    )(pm, pm, pm, state, conv_w)


def _mlstm_kernel(q_ref, k_ref, v_ref, o_ref, gif_ref, c0_ref, n0_ref, m0_ref, ng_ref,
                  zm_ref, c1_ref, n1_ref, m1_ref, c_s, n_s, m_s, *, heads, dqk, dv, precise):
    ci = pl.program_id(1)
    lc = q_ref.shape[1]

    @pl.when(ci == 0)
    def _():
        c_s[...] = c0_ref[0]
        n_s[...] = n0_ref[0]
        m_s[...] = m0_ref[0]

    gif = gif_ref[0]
    lf = jnp.minimum(gif, 0.0) - jnp.log(1.0 + jnp.exp(-jnp.abs(gif)))
    row = lax.broadcasted_iota(jnp.int32, gif.shape, 0)
    fc = lf
    s = 1
    while s < lc:
        fc = fc + jnp.where(row >= s, pltpu.roll(fc, s, 0), 0.0)
        s *= 2
    gif_t = gif.T
    fc_t = fc.T
    tri = lax.broadcasted_iota(jnp.int32, (lc, lc), 0) >= lax.broadcasted_iota(jnp.int32, (lc, lc), 1)
    scale = dqk ** -0.5

    for h in range(heads):
        ig_col = gif[:, h:h + 1]
        ig_row = gif_t[h:h + 1, :]
        f_col = fc[:, heads + h:heads + h + 1]
        f_row = fc_t[heads + h:heads + h + 1, :]
        m_prev = m_s[:, h:h + 1]
        qf = q_ref[0, :, h * dqk:(h + 1) * dqk]
        kf = k_ref[0, :, h * dqk:(h + 1) * dqk] * scale
        vf = v_ref[0, :, h * dv:(h + 1) * dv]
        c_h = c_s[h]
        n_h = n_s[h:h + 1, :]

        d = jnp.where(tri, f_col - f_row + ig_row, -jnp.inf)
        inter = f_col + m_prev
        m_t = jnp.maximum(inter, jnp.max(d, axis=-1, keepdims=True))
        w_inter = jnp.exp(inter - m_t)
        p = jnp.exp(d - m_t)
        sm = _mm(qf, kf, precise, _NT) * p
        num = w_inter * _mm(qf, c_h, precise) + _mm(sm, vf, precise)
        den = w_inter * jnp.sum(qf * n_h, axis=-1, keepdims=True) + jnp.sum(sm, axis=-1, keepdims=True)
        hh = num / jnp.maximum(jnp.abs(den), jnp.exp(-m_t))

        f_last = f_col[lc - 1:lc, :]
        m_new = m_t[lc - 1:lc, :]
        decay = jnp.exp(f_last + m_prev - m_new)
        w_s = jnp.exp(f_last - f_col + ig_col - m_new)
        kw = kf * w_s
        c_s[h] = decay * c_h + _mm(kw, vf, precise, _TN)
        n_s[h:h + 1, :] = decay * n_h + jnp.sum(kw, axis=0, keepdims=True)
        m_s[:, h:h + 1] = m_new

        hn = hh * lax.rsqrt(jnp.mean(hh * hh, axis=-1, keepdims=True) + EPS) * ng_ref[:, h * dv:(h + 1) * dv]
        zm_ref[0, :, h * dv:(h + 1) * dv] = (hn * jax.nn.sigmoid(o_ref[0, :, h * dv:(h + 1) * dv])).astype(zm_ref.dtype)

    @pl.when(ci == pl.num_programs(1) - 1)
    def _():
        c1_ref[0] = c_s[...]
        n1_ref[0] = n_s[...]
        m1_ref[0] = m_s[...]


def _mlstm_mixer(pm, gif, c0, n0, m0, norm_g, off_q, heads, dqk, dv, precise):
    b, l, _ = pm.shape
    lc = _tile(l, MLSTM_CHUNK, SUBLANES)
    wq, wv = heads * dqk, heads * dv
    off_k, off_v, off_o = off_q + wq, off_q + 2 * wq, off_q + 2 * wq + wv
    assert off_q % wq == 0 and off_v % wv == 0 and off_o % wv == 0
    kern = functools.partial(_mlstm_kernel, heads=heads, dqk=dqk, dv=dv, precise=precise)
    return pl.pallas_call(
        kern,
        grid=(b, l // lc),
        in_specs=[
            pl.BlockSpec((1, lc, wq), lambda i, j: (i, j, off_q // wq)),
            pl.BlockSpec((1, lc, wq), lambda i, j: (i, j, off_k // wq)),
            pl.BlockSpec((1, lc, wv), lambda i, j: (i, j, off_v // wv)),
            pl.BlockSpec((1, lc, wv), lambda i, j: (i, j, off_o // wv)),
            pl.BlockSpec((1, lc, LANES), lambda i, j: (i, j, 0)),
            pl.BlockSpec((1, heads, dqk, dv), lambda i, j: (i, 0, 0, 0)),
            pl.BlockSpec((1, heads, dqk), lambda i, j: (i, 0, 0)),
            pl.BlockSpec((1, 1, heads), lambda i, j: (i, 0, 0)),
            pl.BlockSpec((1, wv), lambda i, j: (0, 0)),
        ],
        out_specs=[
            pl.BlockSpec((1, lc, wv), lambda i, j: (i, j, 0)),
            pl.BlockSpec((1, heads, dqk, dv), lambda i, j: (i, 0, 0, 0)),
            pl.BlockSpec((1, heads, dqk), lambda i, j: (i, 0, 0)),
            pl.BlockSpec((1, 1, heads), lambda i, j: (i, 0, 0)),
        ],
        out_shape=[
            jax.ShapeDtypeStruct((b, l, wv), f32 if precise else bf16),
            jax.ShapeDtypeStruct((b, heads, dqk, dv), f32),
            jax.ShapeDtypeStruct((b, heads, dqk), f32),
            jax.ShapeDtypeStruct((b, 1, heads), f32),
        ],
        scratch_shapes=[pltpu.VMEM((heads, dqk, dv), f32), pltpu.VMEM((heads, dqk), f32), pltpu.VMEM((1, heads), f32)],
        compiler_params=_params("arbitrary", "arbitrary"),
        name="mlstm_mixer",
    )(pm, pm, pm, pm, gif, c0, n0, m0.reshape(b, 1, heads), norm_g.reshape(1, wv))


def _merge_kernel(z_ref, zm_ref, gc_ref, gm_ref, wc_ref, wm_ref, o_ref, *, precise):
    if precise:
        pc = _mm(z_ref[...], wc_ref[...], True)
        pm = _mm(zm_ref[...], wm_ref[...], True)
    else:
        pc = jnp.dot(z_ref[...], wc_ref[...], preferred_element_type=f32)
        pm = jnp.dot(zm_ref[...], wm_ref[...], preferred_element_type=f32)
    o_ref[...] = (jax.nn.sigmoid(gc_ref[...]) * pc + jax.nn.sigmoid(gm_ref[...]) * pm).astype(o_ref.dtype)


def _merge(z, zm, g, wc, wm, precise):
    r, dc = z.shape
    dm = zm.shape[1]
    d = wc.shape[1]
    tn = _tile(d, 512)
    tm = _tile(r, 1024, SUBLANES)
    nn = d // tn
    return pl.pallas_call(
        functools.partial(_merge_kernel, precise=precise),
        grid=(nn, r // tm),
        in_specs=[
            pl.BlockSpec((tm, dc), lambda j, i: (i, 0)),
            pl.BlockSpec((tm, dm), lambda j, i: (i, 0)),
            pl.BlockSpec((tm, tn), lambda j, i: (i, j)),
            pl.BlockSpec((tm, tn), lambda j, i: (i, j + nn)),
            pl.BlockSpec((dc, tn), lambda j, i: (0, j)),
            pl.BlockSpec((dm, tn), lambda j, i: (0, j)),
        ],
        out_specs=pl.BlockSpec((tm, tn), lambda j, i: (i, j)),
        out_shape=jax.ShapeDtypeStruct((r, d), f32 if precise else bf16),
        compiler_params=_params("arbitrary", "arbitrary"),
        name="merge",
    )(z, zm, g, g, wc, wm)


def _pack_bf16_pairs(x):
    half = x.shape[1] // 2
    hi = lax.bitcast_convert_type(x[:, :half].astype(bf16).astype(f32), jnp.uint32)
    lo = lax.bitcast_convert_type(x[:, half:].astype(bf16).astype(f32), jnp.uint32)
    return hi | (lo >> 16)


def _unpack_bf16_pairs(u):
    hi = lax.bitcast_convert_type(u & jnp.uint32(0xFFFF0000), f32).astype(bf16)
    lo = lax.bitcast_convert_type(u << 16, f32).astype(bf16)
    return hi, lo


def _post_kernel(mg_ref, x_ref, gt_ref, g_ref, sc_ref, sh_ref, wr_ref, br_ref, cnt0_ref, *rest, has_mix):
    if has_mix:
        x1_ref, h2_ref, plan_ref, tg_ref, cnt_ref, run_ref = rest
    else:
        wo_ref, x1_ref, h2_ref, plan_ref, tg_ref, cnt_ref, run_ref = rest
    bb, tl, d = x_ref.shape
    rows = bb * tl

    @pl.when((pl.program_id(0) == 0) & (pl.program_id(1) == 0))
    def _():
        run_ref[...] = cnt0_ref[...]

    if has_mix:
        mix = mg_ref[...]
    else:
        mix = jnp.dot(mg_ref[...].reshape(rows, d), wo_ref[...], preferred_element_type=f32).reshape(bb, tl, d)
    x1 = x_ref[...] + gt_ref[...] * mix
    x1_ref[...] = x1
    y = x1 * lax.rsqrt(jnp.mean(x1 * x1, axis=-1, keepdims=True) + EPS) * g_ref[...]
    h2 = (y * (1.0 + sc_ref[...]) + sh_ref[...]).reshape(rows, d)
    packed = _pack_bf16_pairs(h2)
    n_slabs = (d // 2) // LANES
    for s in range(n_slabs):
        h2_ref[pl.ds(s, rows, stride=n_slabs), :] = packed[:, s * LANES:(s + 1) * LANES]
    lg = _mm(h2, wr_ref[...], True) + br_ref[...]
    lane = lax.broadcasted_iota(jnp.int32, lg.shape, 1)
    vals, idxs = [], []
    for _ in range(TOP_K):
        mk = jnp.max(lg, axis=-1, keepdims=True)
        ik = jnp.min(jnp.where(lg == mk, lane, LANES), axis=-1, keepdims=True)
        vals.append(mk)
        idxs.append(ik)
        lg = jnp.where(lane == ik, -jnp.inf, lg)
    es = [jnp.exp(v - vals[0]) for v in vals]
    tot = es[0]
    for e in es[1:]:
        tot = tot + e

    sel = [(lane == ik).astype(f32) for ik in idxs]
    picked = sel[0]
    for s in sel[1:]:
        picked = picked + s
    earlier = lax.broadcasted_iota(jnp.int32, (rows, rows), 1) < lax.broadcasted_iota(jnp.int32, (rows, rows), 0)
    before = run_ref[...] + jnp.dot(earlier.astype(bf16), picked.astype(bf16), preferred_element_type=f32)
    run_ref[...] = run_ref[...] + jnp.sum(picked, axis=0, keepdims=True)
    cnt_ref[...] = run_ref[...]

    plan = jnp.zeros(lg.shape, jnp.int32)
    tg = jnp.zeros(lg.shape, f32)
    for k in range(TOP_K):
        rank = jnp.sum(sel[k] * before, axis=-1, keepdims=True).astype(jnp.int32)
        plan = jnp.where(lane == k, idxs[k], plan)
        plan = jnp.where(lane == TOP_K + k, rank, plan)
        tg = jnp.where(lane == k, es[k] / tot, tg)
    plan_ref[...] = plan.reshape(bb, tl, LANES)
    tg_ref[...] = tg.reshape(bb, tl, LANES)


def _post(mg, x, gt, g, sc, sh, wr, br, wo, cnt0, bb, tl):
    b, l, d = x.shape
    row = lambda i, j: (i, j, 0)
    per_b = lambda i, j: (i, 0, 0)
    const2 = lambda i, j: (0, 0)
    in_specs = [
        pl.BlockSpec((bb, tl, d), row),
        pl.BlockSpec((bb, tl, d), row),
        pl.BlockSpec((bb, 1, d), per_b),
        pl.BlockSpec((1, 1, d), lambda i, j: (0, 0, 0)),
        pl.BlockSpec((bb, 1, d), per_b),
        pl.BlockSpec((bb, 1, d), per_b),
        pl.BlockSpec((d, LANES), const2),
        pl.BlockSpec((1, LANES), const2),
        pl.BlockSpec((1, LANES), const2),
    ]
    args = [mg, x, gt, g.reshape(1, 1, d), sc, sh, wr, br, cnt0]
    if wo is not None:
        in_specs.append(pl.BlockSpec((d, d), const2))
        args.append(wo)
    return pl.pallas_call(
        functools.partial(_post_kernel, has_mix=wo is None),
        grid=(b // bb, l // tl),
        in_specs=in_specs,
        out_specs=[
            pl.BlockSpec((bb, tl, d), row),
            pl.BlockSpec((bb * tl * (d // 2 // LANES), LANES), lambda i, j: (i * (l // tl) + j, 0)),
            pl.BlockSpec((bb, tl, LANES), row),
            pl.BlockSpec((bb, tl, LANES), row),
            pl.BlockSpec((1, LANES), const2),
        ],
        out_shape=[
            jax.ShapeDtypeStruct((b, l, d), f32),
            jax.ShapeDtypeStruct((b * l * (d // 2 // LANES), LANES), jnp.uint32),
            jax.ShapeDtypeStruct((b, l, LANES), jnp.int32),
            jax.ShapeDtypeStruct((b, l, LANES), f32),
            jax.ShapeDtypeStruct((1, LANES), f32),
        ],
        scratch_shapes=[pltpu.VMEM((1, LANES), f32)],
        compiler_params=_params("arbitrary", "arbitrary"),
        name="post_mixer",
    )(*args)


def _item_copy(src_hbm, item, dst, i, sem, rows_per):
    s = pl.multiple_of(item * rows_per, rows_per)
    t = pl.multiple_of(i * rows_per, rows_per)
    return pltpu.make_async_copy(src_hbm.at[pl.ds(s, rows_per)], dst.at[pl.ds(t, rows_per)], sem)


def _start_gather(idx_ref, n_groups, src_hbm, dst, sem, rows_per=1):
    def group(g, c):
        for u in range(DMA_UNROLL):
            i = g * DMA_UNROLL + u
            _item_copy(src_hbm, idx_ref[0, 0, i], dst, i, sem, rows_per).start()
        return c

    lax.fori_loop(0, n_groups, group, 0)


def _wait_gather(n_groups, src_hbm, dst, sem, rows_per=1):
    def group(g, c):
        for _ in range(DMA_UNROLL):
            _item_copy(src_hbm, 0, dst, 0, sem, rows_per).wait()
        return c

    lax.fori_loop(0, n_groups, group, 0)


def _dispatch_kernel(tot_ref, grp_ref, tok_ref, nxt_ref, h2_ref, xs_ref, buf_ref, sem, *, rows_per):
    j = pl.program_id(0)
    slot = j % 2

    @pl.when(j == 0)
    def _():
        buf_ref[...] = jnp.zeros_like(buf_ref)
        _start_gather(tok_ref, grp_ref[0], h2_ref, buf_ref.at[0], sem.at[0], rows_per)

    @pl.when(j + 1 < tot_ref[0])
    def _():
        _start_gather(nxt_ref, grp_ref[j + 1], h2_ref, buf_ref.at[1 - slot], sem.at[1 - slot], rows_per)

    @pl.when(j < tot_ref[0])
    def _():
        _wait_gather(grp_ref[j], h2_ref, buf_ref.at[slot], sem.at[slot], rows_per)
        xs_ref[...] = buf_ref[slot]

    @pl.when(j >= tot_ref[0])
    def _():
        xs_ref[...] = jnp.zeros_like(xs_ref)


def _dispatch(h2, row_tok, total, groups, tm, n_tiles, rows_per):
    assert tm % DMA_UNROLL == 0
    tok3 = row_tok.reshape(n_tiles, 1, tm)
    blk = tm * rows_per
    return pl.pallas_call(
        functools.partial(_dispatch_kernel, rows_per=rows_per),
        grid_spec=pltpu.PrefetchScalarGridSpec(
            num_scalar_prefetch=2,
            grid=(n_tiles,),
            in_specs=[
                pl.BlockSpec((1, 1, tm), lambda j, *_: (j, 0, 0), memory_space=pltpu.SMEM),
                pl.BlockSpec((1, 1, tm), lambda j, *_: (jnp.minimum(j + 1, n_tiles - 1), 0, 0),
                             memory_space=pltpu.SMEM),
                pl.BlockSpec(memory_space=pl.ANY),
            ],
            out_specs=pl.BlockSpec((blk, LANES), lambda j, *_: (j, 0)),
            scratch_shapes=[pltpu.VMEM((2, blk, LANES), h2.dtype), pltpu.SemaphoreType.DMA((2,))],
        ),
        out_shape=jax.ShapeDtypeStruct((n_tiles * blk, LANES), h2.dtype),
        compiler_params=_params("arbitrary"),
        name="moe_dispatch",
    )(total, groups, tok3, tok3, h2)


def _expert_weight_ring(first_ref, ta_ref, nact_ref, n_outer, copies, convert):
    o, j = pl.program_id(0), pl.program_id(1)

    @pl.when(first_ref[j] == 1)
    def _():
        a = ta_ref[j]
        n_act = nact_ref[0]
        q = o * n_act + a
        slot = q % 2

        @pl.when(q == 0)
        def _():
            for c in copies(o, a, slot):
                c.start()

        wrap = a + 1 >= n_act
        o2 = jnp.where(wrap, o + 1, o)
        a2 = jnp.where(wrap, 0, a + 1)

        @pl.when(o2 < n_outer)
        def _():
            for c in copies(o2, a2, 1 - slot):
                c.start()

        for c in copies(o, a, slot):
            c.wait()
        convert(slot)


def _moe_up_kernel(te_ref, first_ref, tot_ref, ta_ref, swe_ref, nact_ref, xs_ref, bg_ref, bu_ref, wgu_hbm,
                   act_ref, wst, wg_s, wu_s, sem, *, nf, dff):
    j = pl.program_id(1)
    tf = wg_s.shape[1]

    def copies(f, a, slot):
        e = swe_ref[a]
        col = pl.multiple_of(f * tf, tf)
        return (pltpu.make_async_copy(wgu_hbm.at[e, :, pl.ds(col, tf)], wst.at[slot, 0], sem.at[slot, 0]),
                pltpu.make_async_copy(wgu_hbm.at[e, :, pl.ds(pl.multiple_of(dff + col, tf), tf)], wst.at[slot, 1],
                                      sem.at[slot, 1]))

    def convert(slot):
        wg_s[...] = wst[slot, 0].astype(bf16)
        wu_s[...] = wst[slot, 1].astype(bf16)

    @pl.when(j < tot_ref[0])
    def _():
        _expert_weight_ring(first_ref, ta_ref, nact_ref, nf, copies, convert)
        half = wg_s.shape[0] // 2
        n_slabs = half // LANES
        tm = xs_ref.shape[0] // n_slabs
        slabs = [_unpack_bf16_pairs(xs_ref[pl.ds(s, tm, stride=n_slabs), :]) for s in range(n_slabs)]
        xa = jnp.concatenate([hi for hi, _ in slabs], axis=1)
        xb = jnp.concatenate([lo for _, lo in slabs], axis=1)
        mm = functools.partial(jnp.dot, preferred_element_type=f32)
        gate = mm(xa, wg_s[:half, :]) + mm(xb, wg_s[half:, :]) + bg_ref[0]
        up = mm(xa, wu_s[:half, :]) + mm(xb, wu_s[half:, :]) + bu_ref[0]
        glu = jnp.minimum(gate, SWIGLU_LIMIT)
        lin = jnp.clip(up, -SWIGLU_LIMIT, SWIGLU_LIMIT)
        act_ref[...] = (glu * jax.nn.sigmoid(SWIGLU_ALPHA * glu) * (lin + 1.0)).astype(act_ref.dtype)

    @pl.when(j >= tot_ref[0])
    def _():
        act_ref[...] = jnp.zeros_like(act_ref)


def _moe_up(xs, w_gu, b_gu, sched, tm, n_tiles, dff):
    experts, d, _ = w_gu.shape
    rows_per = d // 2 // LANES
    assert xs.shape == (n_tiles * tm * rows_per, LANES)
    tf = _tile(dff, 512)
    nf = dff // tf
    last = lambda j, tot: jnp.minimum(j, tot[0] - 1)
    return pl.pallas_call(
        functools.partial(_moe_up_kernel, nf=nf, dff=dff),
        grid_spec=pltpu.PrefetchScalarGridSpec(
            num_scalar_prefetch=6,
            grid=(nf, n_tiles),
            in_specs=[
                pl.BlockSpec((tm * rows_per, LANES), lambda f, j, te, fi, tot, *_: (last(j, tot), 0)),
                pl.BlockSpec((1, 1, tf), lambda f, j, te, *_: (te[j], 0, f)),
                pl.BlockSpec((1, 1, tf), lambda f, j, te, *_: (te[j], 0, f + nf)),
                pl.BlockSpec(memory_space=pl.ANY),
            ],
            out_specs=pl.BlockSpec((tm, tf), lambda f, j, *_: (j, f)),
            scratch_shapes=[pltpu.VMEM((2, 2, d, tf), f32), pltpu.VMEM((d, tf), bf16), pltpu.VMEM((d, tf), bf16),
                            pltpu.SemaphoreType.DMA((2, 2))],
        ),
        out_shape=jax.ShapeDtypeStruct((n_tiles * tm, dff), bf16),
        compiler_params=_params("arbitrary", "arbitrary"),
        name="moe_up",
    )(*sched, xs, b_gu.reshape(experts, 1, -1), b_gu.reshape(experts, 1, -1), w_gu)


def _moe_down_kernel(te_ref, first_ref, tot_ref, ta_ref, swe_ref, nact_ref, act_ref, b_ref, wdn_hbm,
                     y_ref, wst, w_s, sem, *, nn):
    j = pl.program_id(1)
    tn = w_s.shape[1]

    def copies(n, a, slot):
        col = pl.multiple_of(n * tn, tn)
        return (pltpu.make_async_copy(wdn_hbm.at[swe_ref[a], :, pl.ds(col, tn)], wst.at[slot], sem.at[slot]),)

    def convert(slot):
        w_s[...] = wst[slot].astype(bf16)

    @pl.when(j < tot_ref[0])
    def _():
        _expert_weight_ring(first_ref, ta_ref, nact_ref, nn, copies, convert)
        y_ref[...] = jnp.dot(act_ref[...], w_s[...], preferred_element_type=f32) + b_ref[0]

    @pl.when(j >= tot_ref[0])
    def _():
        y_ref[...] = jnp.zeros_like(y_ref)


def _moe_down(act, w_dn, b_dn, sched, tm, n_tiles):
    dff = act.shape[1]
    experts, _, d = w_dn.shape
    tn = _tile(d, 1024)
    nn = d // tn
    last = lambda j, tot: jnp.minimum(j, tot[0] - 1)
    return pl.pallas_call(
        functools.partial(_moe_down_kernel, nn=nn),
        grid_spec=pltpu.PrefetchScalarGridSpec(
            num_scalar_prefetch=6,
            grid=(nn, n_tiles),
            in_specs=[
                pl.BlockSpec((tm, dff), lambda n, j, te, fi, tot, *_: (last(j, tot), 0)),
                pl.BlockSpec((1, 1, tn), lambda n, j, te, *_: (te[j], 0, n)),
                pl.BlockSpec(memory_space=pl.ANY),
            ],
            out_specs=pl.BlockSpec((tm, tn), lambda n, j, *_: (j, n)),
            scratch_shapes=[pltpu.VMEM((2, dff, tn), f32), pltpu.VMEM((dff, tn), bf16), pltpu.SemaphoreType.DMA((2,))],
        ),
        out_shape=jax.ShapeDtypeStruct((n_tiles * tm, d), f32),
        compiler_params=_params("arbitrary", "arbitrary"),
        name="moe_down",
    )(*sched, act, b_dn.reshape(experts, 1, d), w_dn)


def _combine_kernel(dest_ref, nxt_ref, x1_ref, tg_ref, gt_ref, fg_ref, yb_ref, o_ref, buf_ref, sem, *, final_norm):
    tl = x1_ref.shape[1]
    n = TOP_K * tl
    step = pl.program_id(0) * pl.num_programs(1) + pl.program_id(1)
    n_steps = pl.num_programs(0) * pl.num_programs(1)
    slot = step % 2

    @pl.when(step == 0)
    def _():
        _start_gather(dest_ref, n // DMA_UNROLL, yb_ref, buf_ref.at[0], sem.at[0])

    @pl.when(step + 1 < n_steps)
    def _():
        _start_gather(nxt_ref, n // DMA_UNROLL, yb_ref, buf_ref.at[1 - slot], sem.at[1 - slot])

    _wait_gather(n // DMA_UNROLL, yb_ref, buf_ref.at[slot], sem.at[slot])
    tg = tg_ref[0]
    moe = tg[:, 0:1] * buf_ref[slot, 0:tl, :]
    for k in range(1, TOP_K):
        moe = moe + tg[:, k:k + 1] * buf_ref[slot, k * tl:(k + 1) * tl, :]
    xo = x1_ref[0] + gt_ref[0] * moe
    if final_norm:
        xo = xo * lax.rsqrt(jnp.mean(xo * xo, axis=-1, keepdims=True) + EPS) * fg_ref[...]
    o_ref[0] = xo


def _combine(x1, tg, gt, fg, yb, dest, final_norm):
    b, l, d = x1.shape
    tl = _tile(l, 128, SUBLANES)
    nl = l // tl
    dest_t = dest.reshape(b * nl, tl, TOP_K).transpose(0, 2, 1).reshape(b * nl, 1, TOP_K * tl)
    return pl.pallas_call(
        functools.partial(_combine_kernel, final_norm=final_norm),
        grid=(b, nl),
        in_specs=[
            pl.BlockSpec((1, 1, TOP_K * tl), lambda i, j: (i * nl + j, 0, 0), memory_space=pltpu.SMEM),
            pl.BlockSpec((1, 1, TOP_K * tl), lambda i, j: (jnp.minimum(i * nl + j + 1, b * nl - 1), 0, 0),
                         memory_space=pltpu.SMEM),
            pl.BlockSpec((1, tl, d), lambda i, j: (i, j, 0)),
            pl.BlockSpec((1, tl, LANES), lambda i, j: (i, j, 0)),
            pl.BlockSpec((1, 1, d), lambda i, j: (i, 0, 0)),
            pl.BlockSpec((1, d), lambda i, j: (0, 0)),
            pl.BlockSpec(memory_space=pl.ANY),
        ],
        out_specs=pl.BlockSpec((1, tl, d), lambda i, j: (i, j, 0)),
        out_shape=jax.ShapeDtypeStruct((b, l, d), f32),
        scratch_shapes=[pltpu.VMEM((2, TOP_K * tl, d), f32), pltpu.SemaphoreType.DMA((2,))],
        compiler_params=_params("arbitrary", "arbitrary"),
        name="moe_combine",
    )(dest_t, dest_t, x1, tg, gt, fg.reshape(1, d), yb)


def _lookup(table, idx):
    hit = idx[..., None] == jnp.arange(table.shape[0], dtype=jnp.int32)
    return jnp.sum(jnp.where(hit, table, 0), axis=-1)


def _count_le(ends, x):
    return jnp.sum((ends <= x[..., None]).astype(jnp.int32), axis=-1)


def _routing_plan(plan, counts, experts, tm, n_tiles):
    exp_id, rank = plan[:, :TOP_K], plan[:, TOP_K:]
    na = exp_id.size
    padded = ((counts + tm - 1) // tm) * tm
    pends = jnp.cumsum(padded)
    pstart = pends - padded
    total = pends[-1] // tm
    dest = (_lookup(pstart, exp_id) + rank).reshape(-1)
    tile_id = jnp.minimum(jnp.arange(n_tiles, dtype=jnp.int32), total - 1)
    tile_e = jnp.minimum(_count_le(pends, tile_id * tm), experts - 1)
    first = jnp.concatenate([jnp.ones((1,), jnp.int32), (tile_e[1:] != tile_e[:-1]).astype(jnp.int32)])
    active = (counts > 0).astype(jnp.int32)
    act_idx = jnp.cumsum(active) - 1
    n_active = jnp.sum(active).reshape(1)
    slots = jnp.arange(experts, dtype=jnp.int32)
    active_e = jnp.sum(jnp.where((act_idx[None, :] == slots[:, None]) & (active[None, :] == 1), slots[None, :], 0), axis=1)
    sched = (tile_e, first, total.reshape(1), _lookup(act_idx, tile_e), active_e, n_active)
    tile_rows = jnp.clip(_lookup(pstart + counts, tile_e) - tile_id * tm, 0, tm)
    groups = (tile_rows + DMA_UNROLL - 1) // DMA_UNROLL
    n_pad = n_tiles * tm - na
    gaps = padded - counts
    gap_ends = jnp.cumsum(gaps)
    p = jnp.arange(n_pad, dtype=jnp.int32)
    pe = jnp.minimum(_count_le(gap_ends, p), experts - 1)
    in_gap = p < gap_ends[-1]
    pad_pos = jnp.where(in_gap, _lookup(pstart + counts - (gap_ends - gaps), pe) + p, pends[-1] + (p - gap_ends[-1]))
    tok = jnp.arange(na, dtype=jnp.int32) // TOP_K
    _, row_tok = lax.sort((jnp.concatenate([dest, pad_pos]), jnp.concatenate([tok, jnp.zeros((n_pad,), jnp.int32)])),
                          num_keys=1)
    return dest, row_tok, groups, sched


def _layer_mixers(x, mods, conv_state, c0, n0, m0, cnt0, lp, dims, precise):
    b, l, d = x.shape
    sh1, sc1, gt1, sh2, sc2, gt2 = mods
    dc, heads, dqk, dv = dims
    bb = b if l < 256 else 1
    tl = _tile(l, 512, SUBLANES)
    act = f32 if precise else bf16
    h, gif = _norm1(x, lp["norm1_g"], sc1, sh1, lp["w_if"], lp["b_if"], bb, tl, precise)
    n_main = 3 * dc + 2 * heads * dqk + 2 * heads * dv
    pm = _proj(h.reshape(b * l, d), lp["w_in"], lp["b_in"], n_main, precise).reshape(b, l, n_main)
    pg = _proj(h.reshape(b * l, d), lp["w_g"], lp["b_g"], 2 * d, precise)
    z, conv_new = _conv_mixer(pm, conv_state, lp["conv_w"], dc, act)
    zm, c1, n1, m1 = _mlstm_mixer(pm, gif, c0, n0, m0, lp["mlstm_norm_g"], 3 * dc, heads, dqk, dv, precise)
    if precise:
        mg = _merge(z.reshape(b * l, dc), zm.reshape(b * l, heads * dv), pg, lp["w_conv_out"], lp["w_mlstm_out"], True)
        mix = _proj(mg, lp["w_out"], jnp.zeros((d,), f32), d, True)
        post = _post(mix.reshape(b, l, d), x, gt1, lp["norm2_g"], sc2, sh2, lp["w_router"], lp["b_router"],
                     None, cnt0, bb, tl)
    else:
        mg = _merge(z.reshape(b * l, dc), zm.reshape(b * l, heads * dv), pg, lp["w_conv_out_bf"],
                    lp["w_mlstm_out_bf"], False)
        post = _post(mg.reshape(b, l, d), x, gt1, lp["norm2_g"], sc2, sh2, lp["w_router"], lp["b_router"],
                     lp["w_out_bf"], cnt0, bb, tl)
    return (*post, conv_new, c1, n1, m1.reshape(b, heads))


def kernel(x_prompt, x_sample, c_prompt, c_sample, state_conv, state_C, state_n, state_m, norm1_g, norm2_g, final_g,
           w_ada, b_ada, w_in, b_in, conv_w, w_conv_out, w_mlstm_out, mlstm_norm_g, w_out, w_router, b_router,
           w_gate_up, b_gate_up, w_down, b_down):
    depth = w_in.shape[0]
    bp, lp_, d = x_prompt.shape
    bs, ls, _ = x_sample.shape
    dc = conv_w.shape[-1]
    kw = conv_w.shape[1]
    heads, dqk, dv = state_C.shape[2:]
    experts = w_router.shape[-1]
    dff = w_down.shape[2]
    assert experts <= LANES and 2 * heads <= LANES
    dims = (dc, heads, dqk, dv)
    off_i = 3 * dc + 2 * heads * dqk + 2 * heads * dv
    off_g = off_i + 2 * heads

    r_all = bp * lp_ + bs * ls
    tm = MOE_ROWS
    n_tiles = -(-(r_all * TOP_K) // tm) + experts

    yp, ys = x_prompt, x_sample
    c_all = jnp.concatenate([c_prompt, c_sample], axis=0)
    c_rows = -(-c_all.shape[0] // SUBLANES) * SUBLANES
    c_all = jnp.pad(c_all, ((0, c_rows - c_all.shape[0]), (0, 0)))
    outs_p = [[], [], [], []]
    outs_s = [[], [], [], []]
    for l in range(depth):
        lp = {
            "norm1_g": norm1_g[l], "norm2_g": norm2_g[l], "w_in": w_in[l], "b_in": b_in[l],
            "w_g": w_in[l][:, off_g:], "b_g": b_in[l][off_g:],
            "w_if": jnp.pad(w_in[l][:, off_i:off_g], ((0, 0), (0, LANES - 2 * heads))),
            "b_if": jnp.pad(b_in[l][off_i:off_g], (0, LANES - 2 * heads)).reshape(1, LANES),
            "conv_w": conv_w[l], "mlstm_norm_g": mlstm_norm_g[l],
            "w_conv_out": w_conv_out[l], "w_mlstm_out": w_mlstm_out[l], "w_out": w_out[l],
            "w_conv_out_bf": w_conv_out[l].astype(bf16), "w_mlstm_out_bf": w_mlstm_out[l].astype(bf16),
            "w_out_bf": w_out[l].astype(bf16),
            "w_router": jnp.pad(w_router[l], ((0, 0), (0, LANES - experts))),
            "b_router": jnp.pad(b_router[l], (0, LANES - experts), constant_values=-jnp.inf).reshape(1, LANES),
        }
        mod = _ada(c_all, w_ada[l], b_ada[l])
        mods_p = [m[:bp].reshape(bp, 1, d) for m in jnp.split(mod, 6, axis=-1)]
        mods_s = [m[bp:bp + bs].reshape(bs, 1, d) for m in jnp.split(mod, 6, axis=-1)]

        zp = (jnp.zeros((bp, kw - 1, dc), f32), jnp.zeros((bp, heads, dqk, dv), f32),
              jnp.zeros((bp, heads, dqk), f32), jnp.zeros((bp, heads), f32))
        x1p, h2p, plp, tgp, cntp, cp, Cp, nvp, mvp = _layer_mixers(
            yp, mods_p, *zp, jnp.zeros((1, LANES), f32), lp, dims, False)
        x1s, h2s, pls, tgs, cnt, cs, Cs, nvs, mvs = _layer_mixers(
            ys, mods_s, state_conv[l], state_C[l], state_n[l], state_m[l], cntp, lp, dims, True)

        h2 = jnp.concatenate([h2p, h2s], axis=0)
        plan = jnp.concatenate([plp.reshape(bp * lp_, LANES), pls.reshape(bs * ls, LANES)], axis=0)[:, :2 * TOP_K]
        counts = cnt[0, :experts].astype(jnp.int32)
        dest, row_tok, groups, sched = _routing_plan(plan, counts, experts, tm, n_tiles)
        xs = _dispatch(h2, row_tok, sched[2], groups, tm, n_tiles, d // 2 // LANES)
        act = _moe_up(xs, w_gate_up[l], b_gate_up[l], sched, tm, n_tiles, dff)
        yb = _moe_down(act, w_down[l], b_down[l], sched, tm, n_tiles)
        last = l == depth - 1
        yp = _combine(x1p, tgp, mods_p[5], final_g, yb, dest[:bp * lp_ * TOP_K], last)
        ys = _combine(x1s, tgs, mods_s[5], final_g, yb, dest[bp * lp_ * TOP_K:], last)
        for acc, vals in ((outs_p, (cp, Cp, nvp, mvp)), (outs_s, (cs, Cs, nvs, mvs))):
            for a, v in zip(acc, vals):
                a.append(v)
    return (yp, ys, *[jnp.stack(a) for a in outs_p], *[jnp.stack(a) for a in outs_s])
```

```python
import functools

import jax
import jax.numpy as jnp
from jax import lax
from jax.experimental import pallas as pl
from jax.experimental.pallas import tpu as pltpu

EPS = 1e-6
TOP_K = 4
SWIGLU_ALPHA = 1.702
SWIGLU_LIMIT = 7.0

LANES = 128
SUBLANES = 8
VMEM_LIMIT_BYTES = 56 * 1024 * 1024
MOE_ROWS = 512
MLSTM_CHUNK = 256
DMA_UNROLL = 8

f32 = jnp.float32
bf16 = jnp.bfloat16


def _tile(dim, pref, align=LANES):
    if dim <= pref:
        return dim
    t = (pref // align) * align
    while t >= align:
        if dim % t == 0:
            return t
        t -= align
    return dim


def _params(*sem):
    return pltpu.CompilerParams(dimension_semantics=sem, vmem_limit_bytes=VMEM_LIMIT_BYTES)


def _split(a):
    hi = a.astype(bf16)
    return hi, (a - hi.astype(f32)).astype(bf16)


_NN = (((1,), (0,)), ((), ()))
_NT = (((1,), (1,)), ((), ()))
_TN = (((0,), (0,)), ((), ()))


def _dot3(a, b, dims=_NN):
    (ah, al), (bh, bl) = a, b
    d = functools.partial(lax.dot_general, dimension_numbers=dims, preferred_element_type=f32)
    return d(ah, bh) + (d(ah, bl) + d(al, bh))


def _mm(a, b, precise, dims=_NN):
    if precise:
        return _dot3(_split(a), _split(b), dims)
    return lax.dot_general(a.astype(bf16), b.astype(bf16), dims, preferred_element_type=f32)


def _ada_kernel(c_ref, w_ref, b_ref, o_ref):
    c = c_ref[...]
    o_ref[...] = _mm(c * jax.nn.sigmoid(c), w_ref[...], True) + b_ref[...]


def _ada(c, w, b):
    m, d = c.shape
    n = w.shape[1]
    tn = _tile(n, 512)
    return pl.pallas_call(
        _ada_kernel,
        grid=(n // tn,),
        in_specs=[
            pl.BlockSpec((m, d), lambda j: (0, 0)),
            pl.BlockSpec((d, tn), lambda j: (0, j)),
            pl.BlockSpec((1, tn), lambda j: (0, j)),
        ],
        out_specs=pl.BlockSpec((m, tn), lambda j: (0, j)),
        out_shape=jax.ShapeDtypeStruct((m, n), f32),
        compiler_params=_params("arbitrary"),
        name="ada_mod",
    )(c, w, b.reshape(1, n))


def _norm1_kernel(x_ref, g_ref, sc_ref, sh_ref, wif_ref, bif_ref, h_ref, gif_ref, *, precise):
    bb, tl, d = x_ref.shape
    x = x_ref[...]
    y = x * lax.rsqrt(jnp.mean(x * x, axis=-1, keepdims=True) + EPS) * g_ref[...]
    h = y * (1.0 + sc_ref[...]) + sh_ref[...]
    h_ref[...] = h.astype(h_ref.dtype)
    gif = _mm(h.reshape(bb * tl, d), wif_ref[...], precise) + bif_ref[...]
    gif_ref[...] = gif.reshape(bb, tl, LANES)


def _norm1(x, g, sc, sh, wif, bif, bb, tl, precise):
    b, l, d = x.shape
    return pl.pallas_call(
        functools.partial(_norm1_kernel, precise=precise),
        grid=(b // bb, l // tl),
        in_specs=[
            pl.BlockSpec((bb, tl, d), lambda i, j: (i, j, 0)),
            pl.BlockSpec((1, 1, d), lambda i, j: (0, 0, 0)),
            pl.BlockSpec((bb, 1, d), lambda i, j: (i, 0, 0)),
            pl.BlockSpec((bb, 1, d), lambda i, j: (i, 0, 0)),
            pl.BlockSpec((d, LANES), lambda i, j: (0, 0)),
            pl.BlockSpec((1, LANES), lambda i, j: (0, 0)),
        ],
        out_specs=[
            pl.BlockSpec((bb, tl, d), lambda i, j: (i, j, 0)),
            pl.BlockSpec((bb, tl, LANES), lambda i, j: (i, j, 0)),
        ],
        out_shape=[jax.ShapeDtypeStruct((b, l, d), f32 if precise else bf16),
                   jax.ShapeDtypeStruct((b, l, LANES), f32)],
        compiler_params=_params("arbitrary", "arbitrary"),
        name="norm1",
    )(x, g.reshape(1, 1, d), sc, sh, wif, bif)


def _proj_kernel(h_ref, w_ref, b_ref, o_ref, whi_ref, *wlo_ref, precise):
    @pl.when(pl.program_id(1) == 0)
    def _():
        if precise:
            whi_ref[...], wlo_ref[0][...] = _split(w_ref[...])
        else:
            whi_ref[...] = w_ref[...].astype(bf16)

    if precise:
        acc = _dot3(_split(h_ref[...]), (whi_ref[...], wlo_ref[0][...]))
    else:
        acc = jnp.dot(h_ref[...], whi_ref[...], preferred_element_type=f32)
    o_ref[...] = (acc + b_ref[...]).astype(o_ref.dtype)


def _proj(h, w, b, n_cols, precise):
    r, d = h.shape
    tn = _tile(n_cols, 512 if precise else 1024)
    tm = _tile(r, 1024, SUBLANES)
    return pl.pallas_call(
        functools.partial(_proj_kernel, precise=precise),
        grid=(n_cols // tn, r // tm),
        in_specs=[
            pl.BlockSpec((tm, d), lambda j, i: (i, 0)),
            pl.BlockSpec((d, tn), lambda j, i: (0, j)),
            pl.BlockSpec((1, tn), lambda j, i: (0, j)),
        ],
        out_specs=pl.BlockSpec((tm, tn), lambda j, i: (i, j)),
        out_shape=jax.ShapeDtypeStruct((r, n_cols), f32),
        scratch_shapes=[pltpu.VMEM((d, tn), bf16)] * (2 if precise else 1),
        compiler_params=_params("arbitrary", "arbitrary"),
        name="proj",
    )(h, w, b.reshape(1, -1))


def _conv_kernel(bg_ref, cg_ref, xin_ref, st_ref, w_ref, z_ref, ns_ref, carry_ref):
    li = pl.program_id(2)
    tl = bg_ref.shape[1]
    kw = w_ref.shape[0]

    @pl.when(li == 0)
    def _():
        carry_ref[...] = jnp.zeros_like(carry_ref)
        carry_ref[SUBLANES - (kw - 1):, :] = st_ref[0]

    u = cg_ref[0] * xin_ref[0]
    row = lax.broadcasted_iota(jnp.int32, u.shape, 0)
    y = u * w_ref[kw - 1:kw, :]
    for s in range(1, kw):
        us = pltpu.roll(u, s, 0)
        for r in range(s):
            us = jnp.where(row == r, carry_ref[SUBLANES - s + r:SUBLANES - s + r + 1, :], us)
        y = y + us * w_ref[kw - 1 - s:kw - s, :]
    z_ref[0] = (bg_ref[0] * y).astype(z_ref.dtype)
    carry_ref[...] = u[tl - SUBLANES:, :]

    @pl.when(li == pl.num_programs(2) - 1)
    def _():
        ns_ref[0] = u[tl - (kw - 1):, :]


def _conv_mixer(pm, state, conv_w, dc, out_dtype):
    b, l, _ = pm.shape
    kw = conv_w.shape[0]
    tl = _tile(l, 1024, SUBLANES)
    tc = _tile(dc, 512)
    assert tl >= SUBLANES and tl % SUBLANES == 0 and kw - 1 <= SUBLANES and l >= kw - 1
    nc = dc // tc
    return pl.pallas_call(
        _conv_kernel,
        grid=(b, nc, l // tl),
        in_specs=[
            pl.BlockSpec((1, tl, tc), lambda i, c, j: (i, j, c)),
            pl.BlockSpec((1, tl, tc), lambda i, c, j: (i, j, c + nc)),
            pl.BlockSpec((1, tl, tc), lambda i, c, j: (i, j, c + 2 * nc)),
            pl.BlockSpec((1, kw - 1, tc), lambda i, c, j: (i, 0, c)),
            pl.BlockSpec((kw, tc), lambda i, c, j: (0, c)),
        ],
        out_specs=[
            pl.BlockSpec((1, tl, tc), lambda i, c, j: (i, j, c)),
            pl.BlockSpec((1, kw - 1, tc), lambda i, c, j: (i, 0, c)),
        ],
        out_shape=[jax.ShapeDtypeStruct((b, l, dc), out_dtype), jax.ShapeDtypeStruct((b, kw - 1, dc), f32)],
        scratch_shapes=[pltpu.VMEM((SUBLANES, tc), f32)],
        compiler_params=_params("arbitrary", "arbitrary", "arbitrary"),
        name="conv_mixer",
    )(pm, pm, pm, state, conv_w)


def _mlstm_kernel(q_ref, k_ref, v_ref, o_ref, gif_ref, c0_ref, n0_ref, m0_ref, ng_ref,
                  zm_ref, c1_ref, n1_ref, m1_ref, c_s, n_s, m_s, *, heads, dqk, dv, precise):
    ci = pl.program_id(1)
    lc = q_ref.shape[1]

    @pl.when(ci == 0)
    def _():
        c_s[...] = c0_ref[0]
        n_s[...] = n0_ref[0]
        m_s[...] = m0_ref[0]

    gif = gif_ref[0]
    lf = jnp.minimum(gif, 0.0) - jnp.log(1.0 + jnp.exp(-jnp.abs(gif)))
    row = lax.broadcasted_iota(jnp.int32, gif.shape, 0)
    fc = lf
    s = 1
    while s < lc:
        fc = fc + jnp.where(row >= s, pltpu.roll(fc, s, 0), 0.0)
        s *= 2
    gif_t = gif.T
    fc_t = fc.T
    tri = lax.broadcasted_iota(jnp.int32, (lc, lc), 0) >= lax.broadcasted_iota(jnp.int32, (lc, lc), 1)
    scale = dqk ** -0.5

    for h in range(heads):
        ig_col = gif[:, h:h + 1]
        ig_row = gif_t[h:h + 1, :]
        f_col = fc[:, heads + h:heads + h + 1]
        f_row = fc_t[heads + h:heads + h + 1, :]
        m_prev = m_s[:, h:h + 1]
        qf = q_ref[0, :, h * dqk:(h + 1) * dqk]
        kf = k_ref[0, :, h * dqk:(h + 1) * dqk] * scale
        vf = v_ref[0, :, h * dv:(h + 1) * dv]
        c_h = c_s[h]
        n_h = n_s[h:h + 1, :]

        d = jnp.where(tri, f_col - f_row + ig_row, -jnp.inf)
        inter = f_col + m_prev
        m_t = jnp.maximum(inter, jnp.max(d, axis=-1, keepdims=True))
        w_inter = jnp.exp(inter - m_t)
        p = jnp.exp(d - m_t)
        sm = _mm(qf, kf, precise, _NT) * p
        num = w_inter * _mm(qf, c_h, precise) + _mm(sm, vf, precise)
        den = w_inter * jnp.sum(qf * n_h, axis=-1, keepdims=True) + jnp.sum(sm, axis=-1, keepdims=True)
        hh = num / jnp.maximum(jnp.abs(den), jnp.exp(-m_t))

        f_last = f_col[lc - 1:lc, :]
        m_new = m_t[lc - 1:lc, :]
        decay = jnp.exp(f_last + m_prev - m_new)
        w_s = jnp.exp(f_last - f_col + ig_col - m_new)
        kw = kf * w_s
        c_s[h] = decay * c_h + _mm(kw, vf, precise, _TN)
        n_s[h:h + 1, :] = decay * n_h + jnp.sum(kw, axis=0, keepdims=True)
        m_s[:, h:h + 1] = m_new

        hn = hh * lax.rsqrt(jnp.mean(hh * hh, axis=-1, keepdims=True) + EPS) * ng_ref[:, h * dv:(h + 1) * dv]
        zm_ref[0, :, h * dv:(h + 1) * dv] = (hn * jax.nn.sigmoid(o_ref[0, :, h * dv:(h + 1) * dv])).astype(zm_ref.dtype)

    @pl.when(ci == pl.num_programs(1) - 1)
    def _():
        c1_ref[0] = c_s[...]
        n1_ref[0] = n_s[...]
        m1_ref[0] = m_s[...]


def _mlstm_mixer(pm, gif, c0, n0, m0, norm_g, off_q, heads, dqk, dv, precise):
    b, l, _ = pm.shape
    lc = _tile(l, MLSTM_CHUNK, SUBLANES)
    wq, wv = heads * dqk, heads * dv
    off_k, off_v, off_o = off_q + wq, off_q + 2 * wq, off_q + 2 * wq + wv
    assert off_q % wq == 0 and off_v % wv == 0 and off_o % wv == 0
    kern = functools.partial(_mlstm_kernel, heads=heads, dqk=dqk, dv=dv, precise=precise)
    return pl.pallas_call(
        kern,
        grid=(b, l // lc),
        in_specs=[
            pl.BlockSpec((1, lc, wq), lambda i, j: (i, j, off_q // wq)),
            pl.BlockSpec((1, lc, wq), lambda i, j: (i, j, off_k // wq)),
            pl.BlockSpec((1, lc, wv), lambda i, j: (i, j, off_v // wv)),
            pl.BlockSpec((1, lc, wv), lambda i, j: (i, j, off_o // wv)),
            pl.BlockSpec((1, lc, LANES), lambda i, j: (i, j, 0)),
            pl.BlockSpec((1, heads, dqk, dv), lambda i, j: (i, 0, 0, 0)),
            pl.BlockSpec((1, heads, dqk), lambda i, j: (i, 0, 0)),
            pl.BlockSpec((1, 1, heads), lambda i, j: (i, 0, 0)),
            pl.BlockSpec((1, wv), lambda i, j: (0, 0)),
        ],
        out_specs=[
            pl.BlockSpec((1, lc, wv), lambda i, j: (i, j, 0)),
            pl.BlockSpec((1, heads, dqk, dv), lambda i, j: (i, 0, 0, 0)),
            pl.BlockSpec((1, heads, dqk), lambda i, j: (i, 0, 0)),
            pl.BlockSpec((1, 1, heads), lambda i, j: (i, 0, 0)),
        ],
        out_shape=[
            jax.ShapeDtypeStruct((b, l, wv), f32 if precise else bf16),
            jax.ShapeDtypeStruct((b, heads, dqk, dv), f32),
            jax.ShapeDtypeStruct((b, heads, dqk), f32),
            jax.ShapeDtypeStruct((b, 1, heads), f32),
        ],
        scratch_shapes=[pltpu.VMEM((heads, dqk, dv), f32), pltpu.VMEM((heads, dqk), f32), pltpu.VMEM((1, heads), f32)],
        compiler_params=_params("arbitrary", "arbitrary"),
        name="mlstm_mixer",
    )(pm, pm, pm, pm, gif, c0, n0, m0.reshape(b, 1, heads), norm_g.reshape(1, wv))


def _merge_kernel(z_ref, zm_ref, gc_ref, gm_ref, wc_ref, wm_ref, o_ref, *, precise):
    if precise:
        pc = _mm(z_ref[...], wc_ref[...], True)
        pm = _mm(zm_ref[...], wm_ref[...], True)
    else:
        pc = jnp.dot(z_ref[...], wc_ref[...], preferred_element_type=f32)
        pm = jnp.dot(zm_ref[...], wm_ref[...], preferred_element_type=f32)
    o_ref[...] = (jax.nn.sigmoid(gc_ref[...]) * pc + jax.nn.sigmoid(gm_ref[...]) * pm).astype(o_ref.dtype)


def _merge(z, zm, g, wc, wm, precise):
    r, dc = z.shape
    dm = zm.shape[1]
    d = wc.shape[1]
    tn = _tile(d, 512)
    tm = _tile(r, 1024, SUBLANES)
    nn = d // tn
    return pl.pallas_call(
        functools.partial(_merge_kernel, precise=precise),
        grid=(nn, r // tm),
        in_specs=[
            pl.BlockSpec((tm, dc), lambda j, i: (i, 0)),
            pl.BlockSpec((tm, dm), lambda j, i: (i, 0)),
            pl.BlockSpec((tm, tn), lambda j, i: (i, j)),
            pl.BlockSpec((tm, tn), lambda j, i: (i, j + nn)),
            pl.BlockSpec((dc, tn), lambda j, i: (0, j)),
            pl.BlockSpec((dm, tn), lambda j, i: (0, j)),
        ],
        out_specs=pl.BlockSpec((tm, tn), lambda j, i: (i, j)),
        out_shape=jax.ShapeDtypeStruct((r, d), f32 if precise else bf16),
        compiler_params=_params("arbitrary", "arbitrary"),
        name="merge",
    )(z, zm, g, g, wc, wm)


def _pack_bf16_pairs(x):
    half = x.shape[1] // 2
    hi = lax.bitcast_convert_type(x[:, :half].astype(bf16).astype(f32), jnp.uint32)
    lo = lax.bitcast_convert_type(x[:, half:].astype(bf16).astype(f32), jnp.uint32)
    return hi | (lo >> 16)


def _unpack_bf16_pairs(u):
    hi = lax.bitcast_convert_type(u & jnp.uint32(0xFFFF0000), f32).astype(bf16)
    lo = lax.bitcast_convert_type(u << 16, f32).astype(bf16)
    return hi, lo


def _post_kernel(mg_ref, x_ref, gt_ref, g_ref, sc_ref, sh_ref, wr_ref, br_ref, cnt0_ref, *rest, has_mix):
    if has_mix:
        x1_ref, h2_ref, plan_ref, tg_ref, cnt_ref, run_ref = rest
    else:
        wo_ref, x1_ref, h2_ref, plan_ref, tg_ref, cnt_ref, run_ref = rest
    bb, tl, d = x_ref.shape
    rows = bb * tl

    @pl.when((pl.program_id(0) == 0) & (pl.program_id(1) == 0))
    def _():
        run_ref[...] = cnt0_ref[...]

    if has_mix:
        mix = mg_ref[...]
    else:
        mix = jnp.dot(mg_ref[...].reshape(rows, d), wo_ref[...], preferred_element_type=f32).reshape(bb, tl, d)
    x1 = x_ref[...] + gt_ref[...] * mix
    x1_ref[...] = x1
    y = x1 * lax.rsqrt(jnp.mean(x1 * x1, axis=-1, keepdims=True) + EPS) * g_ref[...]
    h2 = (y * (1.0 + sc_ref[...]) + sh_ref[...]).reshape(rows, d)
    packed = _pack_bf16_pairs(h2)
    n_slabs = (d // 2) // LANES
    for s in range(n_slabs):
        h2_ref[pl.ds(s, rows, stride=n_slabs), :] = packed[:, s * LANES:(s + 1) * LANES]
    lg = _mm(h2, wr_ref[...], True) + br_ref[...]
    lane = lax.broadcasted_iota(jnp.int32, lg.shape, 1)
    vals, idxs = [], []
    for _ in range(TOP_K):
        mk = jnp.max(lg, axis=-1, keepdims=True)
        ik = jnp.min(jnp.where(lg == mk, lane, LANES), axis=-1, keepdims=True)
        vals.append(mk)
        idxs.append(ik)
        lg = jnp.where(lane == ik, -jnp.inf, lg)
    es = [jnp.exp(v - vals[0]) for v in vals]
    tot = es[0]
    for e in es[1:]:
        tot = tot + e

    sel = [(lane == ik).astype(f32) for ik in idxs]
    picked = sel[0]
    for s in sel[1:]:
        picked = picked + s
    earlier = lax.broadcasted_iota(jnp.int32, (rows, rows), 1) < lax.broadcasted_iota(jnp.int32, (rows, rows), 0)
    before = run_ref[...] + jnp.dot(earlier.astype(bf16), picked.astype(bf16), preferred_element_type=f32)
    run_ref[...] = run_ref[...] + jnp.sum(picked, axis=0, keepdims=True)
    cnt_ref[...] = run_ref[...]

    plan = jnp.zeros(lg.shape, jnp.int32)
    tg = jnp.zeros(lg.shape, f32)
    for k in range(TOP_K):
        rank = jnp.sum(sel[k] * before, axis=-1, keepdims=True).astype(jnp.int32)
        plan = jnp.where(lane == k, idxs[k], plan)
        plan = jnp.where(lane == TOP_K + k, rank, plan)
        tg = jnp.where(lane == k, es[k] / tot, tg)
    plan_ref[...] = plan.reshape(bb, tl, LANES)
    tg_ref[...] = tg.reshape(bb, tl, LANES)


def _post(mg, x, gt, g, sc, sh, wr, br, wo, cnt0, bb, tl):
    b, l, d = x.shape
    row = lambda i, j: (i, j, 0)
    per_b = lambda i, j: (i, 0, 0)
    const2 = lambda i, j: (0, 0)
    in_specs = [
        pl.BlockSpec((bb, tl, d), row),
        pl.BlockSpec((bb, tl, d), row),
        pl.BlockSpec((bb, 1, d), per_b),
        pl.BlockSpec((1, 1, d), lambda i, j: (0, 0, 0)),
        pl.BlockSpec((bb, 1, d), per_b),
        pl.BlockSpec((bb, 1, d), per_b),
        pl.BlockSpec((d, LANES), const2),
        pl.BlockSpec((1, LANES), const2),
        pl.BlockSpec((1, LANES), const2),
    ]
    args = [mg, x, gt, g.reshape(1, 1, d), sc, sh, wr, br, cnt0]
    if wo is not None:
        in_specs.append(pl.BlockSpec((d, d), const2))
        args.append(wo)
    return pl.pallas_call(
        functools.partial(_post_kernel, has_mix=wo is None),
        grid=(b // bb, l // tl),
        in_specs=in_specs,
        out_specs=[
            pl.BlockSpec((bb, tl, d), row),
            pl.BlockSpec((bb * tl * (d // 2 // LANES), LANES), lambda i, j: (i * (l // tl) + j, 0)),
            pl.BlockSpec((bb, tl, LANES), row),
            pl.BlockSpec((bb, tl, LANES), row),
            pl.BlockSpec((1, LANES), const2),
        ],
        out_shape=[
            jax.ShapeDtypeStruct((b, l, d), f32),
            jax.ShapeDtypeStruct((b * l * (d // 2 // LANES), LANES), jnp.uint32),
            jax.ShapeDtypeStruct((b, l, LANES), jnp.int32),
            jax.ShapeDtypeStruct((b, l, LANES), f32),
            jax.ShapeDtypeStruct((1, LANES), f32),
        ],
        scratch_shapes=[pltpu.VMEM((1, LANES), f32)],
        compiler_params=_params("arbitrary", "arbitrary"),
        name="post_mixer",
    )(*args)


def _item_copy(src_hbm, item, dst, i, sem, rows_per):
    s = pl.multiple_of(item * rows_per, rows_per)
    t = pl.multiple_of(i * rows_per, rows_per)
    return pltpu.make_async_copy(src_hbm.at[pl.ds(s, rows_per)], dst.at[pl.ds(t, rows_per)], sem)


def _start_gather(idx_ref, n_groups, src_hbm, dst, sem, rows_per=1):
    def group(g, c):
        for u in range(DMA_UNROLL):
            i = g * DMA_UNROLL + u
            _item_copy(src_hbm, idx_ref[0, 0, i], dst, i, sem, rows_per).start()
        return c

    lax.fori_loop(0, n_groups, group, 0)


def _wait_gather(n_groups, src_hbm, dst, sem, rows_per=1):
    def group(g, c):
        for _ in range(DMA_UNROLL):
            _item_copy(src_hbm, 0, dst, 0, sem, rows_per).wait()
        return c

    lax.fori_loop(0, n_groups, group, 0)


def _dispatch_kernel(tot_ref, grp_ref, tok_ref, nxt_ref, h2_ref, xs_ref, buf_ref, sem, *, rows_per):
    j = pl.program_id(0)
    slot = j % 2

    @pl.when(j == 0)
    def _():
        buf_ref[...] = jnp.zeros_like(buf_ref)
        _start_gather(tok_ref, grp_ref[0], h2_ref, buf_ref.at[0], sem.at[0], rows_per)

    @pl.when(j + 1 < tot_ref[0])
    def _():
        _start_gather(nxt_ref, grp_ref[j + 1], h2_ref, buf_ref.at[1 - slot], sem.at[1 - slot], rows_per)

    @pl.when(j < tot_ref[0])
    def _():
        _wait_gather(grp_ref[j], h2_ref, buf_ref.at[slot], sem.at[slot], rows_per)
        xs_ref[...] = buf_ref[slot]

    @pl.when(j >= tot_ref[0])
    def _():
        xs_ref[...] = jnp.zeros_like(xs_ref)


def _dispatch(h2, row_tok, total, groups, tm, n_tiles, rows_per):
    assert tm % DMA_UNROLL == 0
    tok3 = row_tok.reshape(n_tiles, 1, tm)
    blk = tm * rows_per
    return pl.pallas_call(
        functools.partial(_dispatch_kernel, rows_per=rows_per),
        grid_spec=pltpu.PrefetchScalarGridSpec(
            num_scalar_prefetch=2,
            grid=(n_tiles,),
            in_specs=[
                pl.BlockSpec((1, 1, tm), lambda j, *_: (j, 0, 0), memory_space=pltpu.SMEM),
                pl.BlockSpec((1, 1, tm), lambda j, *_: (jnp.minimum(j + 1, n_tiles - 1), 0, 0),
                             memory_space=pltpu.SMEM),
                pl.BlockSpec(memory_space=pl.ANY),
            ],
            out_specs=pl.BlockSpec((blk, LANES), lambda j, *_: (j, 0)),
            scratch_shapes=[pltpu.VMEM((2, blk, LANES), h2.dtype), pltpu.SemaphoreType.DMA((2,))],
        ),
        out_shape=jax.ShapeDtypeStruct((n_tiles * blk, LANES), h2.dtype),
        compiler_params=_params("arbitrary"),
        name="moe_dispatch",
    )(total, groups, tok3, tok3, h2)


def _expert_weight_ring(first_ref, ta_ref, nact_ref, n_outer, copies, convert):
    o, j = pl.program_id(0), pl.program_id(1)

    @pl.when(first_ref[j] == 1)
    def _():
        a = ta_ref[j]
        n_act = nact_ref[0]

        @pl.when((o == 0) & (a == 0))
        def _():
            for c in copies(o, a):
                c.start()

        for c in copies(o, a):
            c.wait()
        convert()

        wrap = a + 1 >= n_act
        o2 = jnp.where(wrap, o + 1, o)
        a2 = jnp.where(wrap, 0, a + 1)

        @pl.when(o2 < n_outer)
        def _():
            for c in copies(o2, a2):
                c.start()


def _moe_up_kernel(te_ref, first_ref, tot_ref, ta_ref, swe_ref, nact_ref, xs_ref, bg_ref, bu_ref, wgu_hbm,
                   act_ref, wst, wg_s, wu_s, sem, *, nf, dff):
    j = pl.program_id(1)
    tf = wg_s.shape[1]

    def copies(f, a):
        e = swe_ref[a]
        col = pl.multiple_of(f * tf, tf)
        return (pltpu.make_async_copy(wgu_hbm.at[e, :, pl.ds(col, tf)], wst.at[0], sem.at[0]),
                pltpu.make_async_copy(wgu_hbm.at[e, :, pl.ds(pl.multiple_of(dff + col, tf), tf)], wst.at[1], sem.at[1]))

    def convert():
        wg_s[...] = wst[0].astype(bf16)
        wu_s[...] = wst[1].astype(bf16)

    @pl.when(j < tot_ref[0])
    def _():
        _expert_weight_ring(first_ref, ta_ref, nact_ref, nf, copies, convert)
        half = wg_s.shape[0] // 2
        n_slabs = half // LANES
        tm = xs_ref.shape[0] // n_slabs
        slabs = [_unpack_bf16_pairs(xs_ref[pl.ds(s, tm, stride=n_slabs), :]) for s in range(n_slabs)]
        xa = jnp.concatenate([hi for hi, _ in slabs], axis=1)
        xb = jnp.concatenate([lo for _, lo in slabs], axis=1)
        mm = functools.partial(jnp.dot, preferred_element_type=f32)
        gate = mm(xa, wg_s[:half, :]) + mm(xb, wg_s[half:, :]) + bg_ref[0]
        up = mm(xa, wu_s[:half, :]) + mm(xb, wu_s[half:, :]) + bu_ref[0]
        glu = jnp.minimum(gate, SWIGLU_LIMIT)
        lin = jnp.clip(up, -SWIGLU_LIMIT, SWIGLU_LIMIT)
        act_ref[...] = (glu * jax.nn.sigmoid(SWIGLU_ALPHA * glu) * (lin + 1.0)).astype(act_ref.dtype)

    @pl.when(j >= tot_ref[0])
    def _():
        act_ref[...] = jnp.zeros_like(act_ref)


def _moe_up(xs, w_gu, b_gu, sched, tm, n_tiles, dff):
    experts, d, _ = w_gu.shape
    rows_per = d // 2 // LANES
    assert xs.shape == (n_tiles * tm * rows_per, LANES)
    tf = _tile(dff, 1024)
    nf = dff // tf
    last = lambda j, tot: jnp.minimum(j, tot[0] - 1)
    return pl.pallas_call(
        functools.partial(_moe_up_kernel, nf=nf, dff=dff),
        grid_spec=pltpu.PrefetchScalarGridSpec(
            num_scalar_prefetch=6,
            grid=(nf, n_tiles),
            in_specs=[
                pl.BlockSpec((tm * rows_per, LANES), lambda f, j, te, fi, tot, *_: (last(j, tot), 0)),
                pl.BlockSpec((1, 1, tf), lambda f, j, te, *_: (te[j], 0, f)),
                pl.BlockSpec((1, 1, tf), lambda f, j, te, *_: (te[j], 0, f + nf)),
                pl.BlockSpec(memory_space=pl.ANY),
            ],
            out_specs=pl.BlockSpec((tm, tf), lambda f, j, *_: (j, f)),
            scratch_shapes=[pltpu.VMEM((2, d, tf), f32), pltpu.VMEM((d, tf), bf16), pltpu.VMEM((d, tf), bf16),
                            pltpu.SemaphoreType.DMA((2,))],
        ),
        out_shape=jax.ShapeDtypeStruct((n_tiles * tm, dff), bf16),
        compiler_params=_params("arbitrary", "arbitrary"),
        name="moe_up",
    )(*sched, xs, b_gu.reshape(experts, 1, -1), b_gu.reshape(experts, 1, -1), w_gu)


def _moe_down_kernel(te_ref, first_ref, tot_ref, ta_ref, swe_ref, nact_ref, act_ref, b_ref, wdn_hbm,
                     y_ref, wst, w_s, sem, *, nn):
    j = pl.program_id(1)
    tn = w_s.shape[1]

    def copies(n, a):
        col = pl.multiple_of(n * tn, tn)
        return (pltpu.make_async_copy(wdn_hbm.at[swe_ref[a], :, pl.ds(col, tn)], wst, sem.at[0]),)

    def convert():
        w_s[...] = wst[...].astype(bf16)

    @pl.when(j < tot_ref[0])
    def _():
        _expert_weight_ring(first_ref, ta_ref, nact_ref, nn, copies, convert)
        y_ref[...] = jnp.dot(act_ref[...], w_s[...], preferred_element_type=f32) + b_ref[0]

    @pl.when(j >= tot_ref[0])
    def _():
        y_ref[...] = jnp.zeros_like(y_ref)


def _moe_down(act, w_dn, b_dn, sched, tm, n_tiles):
    dff = act.shape[1]
    experts, _, d = w_dn.shape
    tn = _tile(d, 2048)
    nn = d // tn
    last = lambda j, tot: jnp.minimum(j, tot[0] - 1)
    return pl.pallas_call(
        functools.partial(_moe_down_kernel, nn=nn),
        grid_spec=pltpu.PrefetchScalarGridSpec(
            num_scalar_prefetch=6,
            grid=(nn, n_tiles),
            in_specs=[
                pl.BlockSpec((tm, dff), lambda n, j, te, fi, tot, *_: (last(j, tot), 0)),
                pl.BlockSpec((1, 1, tn), lambda n, j, te, *_: (te[j], 0, n)),
                pl.BlockSpec(memory_space=pl.ANY),
            ],
            out_specs=pl.BlockSpec((tm, tn), lambda n, j, *_: (j, n)),
            scratch_shapes=[pltpu.VMEM((dff, tn), f32), pltpu.VMEM((dff, tn), bf16), pltpu.SemaphoreType.DMA((1,))],
        ),
        out_shape=jax.ShapeDtypeStruct((n_tiles * tm, d), f32),
        compiler_params=_params("arbitrary", "arbitrary"),
        name="moe_down",
    )(*sched, act, b_dn.reshape(experts, 1, d), w_dn)


def _combine_kernel(dest_ref, nxt_ref, x1_ref, tg_ref, gt_ref, fg_ref, yb_ref, o_ref, buf_ref, sem, *, final_norm):
    tl = x1_ref.shape[1]
    n = TOP_K * tl
    step = pl.program_id(0) * pl.num_programs(1) + pl.program_id(1)
    n_steps = pl.num_programs(0) * pl.num_programs(1)
    slot = step % 2

    @pl.when(step == 0)
    def _():
        _start_gather(dest_ref, n // DMA_UNROLL, yb_ref, buf_ref.at[0], sem.at[0])

    @pl.when(step + 1 < n_steps)
    def _():
        _start_gather(nxt_ref, n // DMA_UNROLL, yb_ref, buf_ref.at[1 - slot], sem.at[1 - slot])

    _wait_gather(n // DMA_UNROLL, yb_ref, buf_ref.at[slot], sem.at[slot])
    tg = tg_ref[0]
    moe = tg[:, 0:1] * buf_ref[slot, 0:tl, :]
    for k in range(1, TOP_K):
        moe = moe + tg[:, k:k + 1] * buf_ref[slot, k * tl:(k + 1) * tl, :]
    xo = x1_ref[0] + gt_ref[0] * moe
    if final_norm:
        xo = xo * lax.rsqrt(jnp.mean(xo * xo, axis=-1, keepdims=True) + EPS) * fg_ref[...]
    o_ref[0] = xo


def _combine(x1, tg, gt, fg, yb, dest, final_norm):
    b, l, d = x1.shape
    tl = _tile(l, 128, SUBLANES)
    nl = l // tl
    dest_t = dest.reshape(b * nl, tl, TOP_K).transpose(0, 2, 1).reshape(b * nl, 1, TOP_K * tl)
    return pl.pallas_call(
        functools.partial(_combine_kernel, final_norm=final_norm),
        grid=(b, nl),
        in_specs=[
            pl.BlockSpec((1, 1, TOP_K * tl), lambda i, j: (i * nl + j, 0, 0), memory_space=pltpu.SMEM),
            pl.BlockSpec((1, 1, TOP_K * tl), lambda i, j: (jnp.minimum(i * nl + j + 1, b * nl - 1), 0, 0),
                         memory_space=pltpu.SMEM),
            pl.BlockSpec((1, tl, d), lambda i, j: (i, j, 0)),
            pl.BlockSpec((1, tl, LANES), lambda i, j: (i, j, 0)),
            pl.BlockSpec((1, 1, d), lambda i, j: (i, 0, 0)),
            pl.BlockSpec((1, d), lambda i, j: (0, 0)),
            pl.BlockSpec(memory_space=pl.ANY),
        ],
        out_specs=pl.BlockSpec((1, tl, d), lambda i, j: (i, j, 0)),
        out_shape=jax.ShapeDtypeStruct((b, l, d), f32),
        scratch_shapes=[pltpu.VMEM((2, TOP_K * tl, d), f32), pltpu.SemaphoreType.DMA((2,))],
        compiler_params=_params("arbitrary", "arbitrary"),
        name="moe_combine",
    )(dest_t, dest_t, x1, tg, gt, fg.reshape(1, d), yb)


def _lookup(table, idx):
    hit = idx[..., None] == jnp.arange(table.shape[0], dtype=jnp.int32)
    return jnp.sum(jnp.where(hit, table, 0), axis=-1)


def _count_le(ends, x):
    return jnp.sum((ends <= x[..., None]).astype(jnp.int32), axis=-1)


def _routing_plan(plan, counts, experts, tm, n_tiles):
    exp_id, rank = plan[:, :TOP_K], plan[:, TOP_K:]
    na = exp_id.size
    padded = ((counts + tm - 1) // tm) * tm
    pends = jnp.cumsum(padded)
    pstart = pends - padded
    total = pends[-1] // tm
    dest = (_lookup(pstart, exp_id) + rank).reshape(-1)
    tile_id = jnp.minimum(jnp.arange(n_tiles, dtype=jnp.int32), total - 1)
    tile_e = jnp.minimum(_count_le(pends, tile_id * tm), experts - 1)
    first = jnp.concatenate([jnp.ones((1,), jnp.int32), (tile_e[1:] != tile_e[:-1]).astype(jnp.int32)])
    active = (counts > 0).astype(jnp.int32)
    act_idx = jnp.cumsum(active) - 1
    n_active = jnp.sum(active).reshape(1)
    slots = jnp.arange(experts, dtype=jnp.int32)
    active_e = jnp.sum(jnp.where((act_idx[None, :] == slots[:, None]) & (active[None, :] == 1), slots[None, :], 0), axis=1)
    sched = (tile_e, first, total.reshape(1), _lookup(act_idx, tile_e), active_e, n_active)
    tile_rows = jnp.clip(_lookup(pstart + counts, tile_e) - tile_id * tm, 0, tm)
    groups = (tile_rows + DMA_UNROLL - 1) // DMA_UNROLL
    n_pad = n_tiles * tm - na
    gaps = padded - counts
    gap_ends = jnp.cumsum(gaps)
    p = jnp.arange(n_pad, dtype=jnp.int32)
    pe = jnp.minimum(_count_le(gap_ends, p), experts - 1)
    in_gap = p < gap_ends[-1]
    pad_pos = jnp.where(in_gap, _lookup(pstart + counts - (gap_ends - gaps), pe) + p, pends[-1] + (p - gap_ends[-1]))
    tok = jnp.arange(na, dtype=jnp.int32) // TOP_K
    _, row_tok = lax.sort((jnp.concatenate([dest, pad_pos]), jnp.concatenate([tok, jnp.zeros((n_pad,), jnp.int32)])),
                          num_keys=1)
    return dest, row_tok, groups, sched


def _layer_mixers(x, mods, conv_state, c0, n0, m0, cnt0, lp, dims, precise):
    b, l, d = x.shape
    sh1, sc1, gt1, sh2, sc2, gt2 = mods
    dc, heads, dqk, dv = dims
    bb = b if l < 256 else 1
    tl = _tile(l, 512, SUBLANES)
    act = f32 if precise else bf16
    h, gif = _norm1(x, lp["norm1_g"], sc1, sh1, lp["w_if"], lp["b_if"], bb, tl, precise)
    n_main = 3 * dc + 2 * heads * dqk + 2 * heads * dv
    pm = _proj(h.reshape(b * l, d), lp["w_in"], lp["b_in"], n_main, precise).reshape(b, l, n_main)
    pg = _proj(h.reshape(b * l, d), lp["w_g"], lp["b_g"], 2 * d, precise)
    z, conv_new = _conv_mixer(pm, conv_state, lp["conv_w"], dc, act)
    zm, c1, n1, m1 = _mlstm_mixer(pm, gif, c0, n0, m0, lp["mlstm_norm_g"], 3 * dc, heads, dqk, dv, precise)
    if precise:
        mg = _merge(z.reshape(b * l, dc), zm.reshape(b * l, heads * dv), pg, lp["w_conv_out"], lp["w_mlstm_out"], True)
        mix = _proj(mg, lp["w_out"], jnp.zeros((d,), f32), d, True)
        post = _post(mix.reshape(b, l, d), x, gt1, lp["norm2_g"], sc2, sh2, lp["w_router"], lp["b_router"],
                     None, cnt0, bb, tl)
    else:
        mg = _merge(z.reshape(b * l, dc), zm.reshape(b * l, heads * dv), pg, lp["w_conv_out_bf"],
                    lp["w_mlstm_out_bf"], False)
        post = _post(mg.reshape(b, l, d), x, gt1, lp["norm2_g"], sc2, sh2, lp["w_router"], lp["b_router"],
                     lp["w_out_bf"], cnt0, bb, tl)
    return (*post, conv_new, c1, n1, m1.reshape(b, heads))


def kernel(x_prompt, x_sample, c_prompt, c_sample, state_conv, state_C, state_n, state_m, norm1_g, norm2_g, final_g,
           w_ada, b_ada, w_in, b_in, conv_w, w_conv_out, w_mlstm_out, mlstm_norm_g, w_out, w_router, b_router,
           w_gate_up, b_gate_up, w_down, b_down):
    depth = w_in.shape[0]
    bp, lp_, d = x_prompt.shape
    bs, ls, _ = x_sample.shape
    dc = conv_w.shape[-1]
    kw = conv_w.shape[1]
    heads, dqk, dv = state_C.shape[2:]
    experts = w_router.shape[-1]
    dff = w_down.shape[2]
    assert experts <= LANES and 2 * heads <= LANES
    dims = (dc, heads, dqk, dv)
    off_i = 3 * dc + 2 * heads * dqk + 2 * heads * dv
    off_g = off_i + 2 * heads

    r_all = bp * lp_ + bs * ls
    tm = MOE_ROWS
    n_tiles = -(-(r_all * TOP_K) // tm) + experts

    yp, ys = x_prompt, x_sample
    c_all = jnp.concatenate([c_prompt, c_sample], axis=0)
    c_rows = -(-c_all.shape[0] // SUBLANES) * SUBLANES
    c_all = jnp.pad(c_all, ((0, c_rows - c_all.shape[0]), (0, 0)))
    outs_p = [[], [], [], []]
    outs_s = [[], [], [], []]
    for l in range(depth):
        lp = {
            "norm1_g": norm1_g[l], "norm2_g": norm2_g[l], "w_in": w_in[l], "b_in": b_in[l],
            "w_g": w_in[l][:, off_g:], "b_g": b_in[l][off_g:],
            "w_if": jnp.pad(w_in[l][:, off_i:off_g], ((0, 0), (0, LANES - 2 * heads))),
            "b_if": jnp.pad(b_in[l][off_i:off_g], (0, LANES - 2 * heads)).reshape(1, LANES),
            "conv_w": conv_w[l], "mlstm_norm_g": mlstm_norm_g[l],
            "w_conv_out": w_conv_out[l], "w_mlstm_out": w_mlstm_out[l], "w_out": w_out[l],
            "w_conv_out_bf": w_conv_out[l].astype(bf16), "w_mlstm_out_bf": w_mlstm_out[l].astype(bf16),
            "w_out_bf": w_out[l].astype(bf16),
            "w_router": jnp.pad(w_router[l], ((0, 0), (0, LANES - experts))),
            "b_router": jnp.pad(b_router[l], (0, LANES - experts), constant_values=-jnp.inf).reshape(1, LANES),
        }
        mod = _ada(c_all, w_ada[l], b_ada[l])
        mods_p = [m[:bp].reshape(bp, 1, d) for m in jnp.split(mod, 6, axis=-1)]
        mods_s = [m[bp:bp + bs].reshape(bs, 1, d) for m in jnp.split(mod, 6, axis=-1)]

        zp = (jnp.zeros((bp, kw - 1, dc), f32), jnp.zeros((bp, heads, dqk, dv), f32),
              jnp.zeros((bp, heads, dqk), f32), jnp.zeros((bp, heads), f32))
        x1p, h2p, plp, tgp, cntp, cp, Cp, nvp, mvp = _layer_mixers(
            yp, mods_p, *zp, jnp.zeros((1, LANES), f32), lp, dims, False)
        x1s, h2s, pls, tgs, cnt, cs, Cs, nvs, mvs = _layer_mixers(
            ys, mods_s, state_conv[l], state_C[l], state_n[l], state_m[l], cntp, lp, dims, True)

        h2 = jnp.concatenate([h2p, h2s], axis=0)
        plan = jnp.concatenate([plp.reshape(bp * lp_, LANES), pls.reshape(bs * ls, LANES)], axis=0)[:, :2 * TOP_K]
        counts = cnt[0, :experts].astype(jnp.int32)
        dest, row_tok, groups, sched = _routing_plan(plan, counts, experts, tm, n_tiles)
        xs = _dispatch(h2, row_tok, sched[2], groups, tm, n_tiles, d // 2 // LANES)
        act = _moe_up(xs, w_gate_up[l], b_gate_up[l], sched, tm, n_tiles, dff)
        yb = _moe_down(act, w_down[l], b_down[l], sched, tm, n_tiles)
        last = l == depth - 1
        yp = _combine(x1p, tgp, mods_p[5], final_g, yb, dest[:bp * lp_ * TOP_K], last)
        ys = _combine(x1s, tgs, mods_s[5], final_g, yb, dest[bp * lp_ * TOP_K:], last)
        for acc, vals in ((outs_p, (cp, Cp, nvp, mvp)), (outs_s, (cs, Cs, nvs, mvs))):
            for a, v in zip(acc, vals):
                a.append(v)
    return (yp, ys, *[jnp.stack(a) for a in outs_p], *[jnp.stack(a) for a in outs_s])
```
